```python
import jax, jax.numpy as jnp
from jax import lax
import numpy as np

D_MODEL = 1024
BATCH = 8
SEQ = 2048
DEPTH = 1
DEC_BATCH = 128
DEC_SEQ = 8
PAST_LEN = 16384
PAGE_SIZE = 128

N_META = 16
MIX_WIDTH = D_MODEL
GDN_WIDTH = MIX_WIDTH // 2
GDN_HEAD_DIM = 128
GDN_HEADS = GDN_WIDTH // GDN_HEAD_DIM
GDN_CHUNK = 64
CONV_W = 4
LRU_WIDTH = MIX_WIDTH - GDN_WIDTH
LRU_BLOCKS = 8
LRU_BLOCK_DIM = LRU_WIDTH // LRU_BLOCKS
LRU_C = 8.0
N_EXPERTS = 256
TOP_K = 8
N_GROUPS = 8
TOPK_GROUPS = 4
EXPERT_DIM = 256
SHARED_DIM = 256
ROUTED_SCALE = 2.5
MOE_BLOCK = 128
DEEPNORM_ALPHA = (2.0 * DEPTH) ** 0.25
DEEPNORM_BETA = (8.0 * DEPTH) ** -0.25
LN_EPS = 1e-5
RMS_EPS = 1e-6
IN_COLS = 4 * GDN_WIDTH + 2 * GDN_HEADS + 2 * LRU_WIDTH

kernel_name = 'hymba_gdn_rglru_deepnorm_moe_step'


def layer_norm(x, g, b):
    xf = x.astype(jnp.float32)
    mu = jnp.mean(xf, axis=-1, keepdims=True)
    var = jnp.mean(jnp.square(xf - mu), axis=-1, keepdims=True)
    return ((xf - mu) * lax.rsqrt(var + LN_EPS) * g.astype(jnp.float32) + b.astype(jnp.float32)).astype(x.dtype)


def rms_norm(x, g):
    xf = x.astype(jnp.float32)
    return xf * lax.rsqrt(jnp.mean(xf * xf, axis=-1, keepdims=True) + RMS_EPS) * g.astype(jnp.float32)


def l2_normalize(x):
    return x * lax.rsqrt(jnp.sum(x * x, axis=-1, keepdims=True) + RMS_EPS)


def causal_conv(x, hist, w):
    t = x.shape[1]
    xx = jnp.concatenate([hist.astype(x.dtype), x], axis=1)
    y = sum(xx[:, j:j + t] * w[j].astype(x.dtype) for j in range(CONV_W))
    return y, xx[:, t:]


def _to_chunks(a, n, chunk, pad):
    if pad:
        a = jnp.pad(a, [(0, 0), (0, pad)] + [(0, 0)] * (a.ndim - 2))
    a = a.reshape((a.shape[0], n, chunk) + a.shape[2:])
    return jnp.moveaxis(jnp.moveaxis(a, 3, 2), 1, 0)


def gdn_chunked(q, k, v, g, beta, s0, chunk):
    b, t, h, dk = q.shape
    dv = v.shape[-1]
    n = -(-t // chunk)
    pad = n * chunk - t
    qc, kc, vc = (_to_chunks(a, n, chunk, pad) for a in (q, k, v))
    gc = jnp.cumsum(_to_chunks(g, n, chunk, pad), axis=-1)
    bc = _to_chunks(beta, n, chunk, pad)
    idx = jnp.arange(chunk)
    causal = idx[:, None] >= idx[None, :]
    strict = idx[:, None] > idx[None, :]
    decay = jnp.exp(jnp.where(causal, gc[..., :, None] - gc[..., None, :], -jnp.inf))
    kk = jnp.einsum('nbhid,nbhjd->nbhij', kc, kc)
    a_mat = jnp.eye(chunk, dtype=jnp.float32) + jnp.where(strict, bc[..., :, None] * kk * decay, 0.0)
    rhs = jnp.concatenate([bc[..., None] * vc, (bc * jnp.exp(gc))[..., None] * kc], axis=-1)
    sol = lax.linalg.triangular_solve(a_mat, rhs, left_side=True, lower=True, unit_diagonal=True)
    tv, tk = sol[..., :dv], sol[..., dv:]
    qk = jnp.einsum('nbhid,nbhjd->nbhij', qc, kc) * decay
    q_dec = qc * jnp.exp(gc)[..., None]
    k_dec = kc * jnp.exp(gc[..., -1:] - gc)[..., None]
    g_tot = jnp.exp(gc[..., -1])

    def step(s, inp):
        tv_c, tk_c, q_c, qk_c, k_c, gt_c = inp
        u = tv_c - jnp.einsum('bhck,bhkv->bhcv', tk_c, s)
        o = jnp.einsum('bhck,bhkv->bhcv', q_c, s) + jnp.einsum('bhij,bhjv->bhiv', qk_c, u)
        s = gt_c[..., None, None] * s + jnp.einsum('bhck,bhcv->bhkv', k_c, u)
        return s, o

    s_final, o = lax.scan(step, s0, (tv, tk, q_dec, qk, k_dec, g_tot))
    o = jnp.moveaxis(jnp.moveaxis(o, 0, 1), 2, 3).reshape(b, n * chunk, h, dv)[:, :t]
    return o, s_final


def mixer(x, s_gdn, conv_gdn, h_lru, conv_lru, w_in, gdn_conv_w, gdn_a_log, gdn_dt_bias, gdn_norm_w,
          lru_conv_w, lru_conv_b, lru_wa, lru_ba, lru_wx, lru_bx, lru_lambda, lru_norm_w, w_out,
          segments, reset_first):
    b, t, _ = x.shape
    f32 = jnp.float32
    G, H, DH, L = GDN_WIDTH, GDN_HEADS, GDN_HEAD_DIM, LRU_WIDTH
    proj = jnp.einsum('btd,dc->btc', x, w_in)
    o0 = 4 * G + 2 * H
    qkv_pre, z = proj[..., :3 * G], proj[..., 3 * G:4 * G]
    beta_pre, a_pre = proj[..., 4 * G:4 * G + H], proj[..., 4 * G + H:o0]
    lru_pre, gate_pre = proj[..., o0:o0 + L], proj[..., o0 + L:]

    qkv, conv_gdn_new = causal_conv(qkv_pre, conv_gdn, gdn_conv_w)
    qkv = jax.nn.silu(qkv.astype(f32))
    q, k, v = (a.reshape(b, t, H, DH) for a in jnp.split(qkv, 3, axis=-1))
    q = l2_normalize(q) * (DH ** -0.5)
    k = l2_normalize(k)
    g = -jnp.exp(gdn_a_log.astype(f32)) * jax.nn.softplus(a_pre.astype(f32) + gdn_dt_bias.astype(f32))
    beta = jax.nn.sigmoid(beta_pre.astype(f32))
    s = s_gdn.astype(f32)
    outs, start = [], 0
    for seg_len, chunk in segments:
        sl = slice(start, start + seg_len)
        o_seg, s = gdn_chunked(q[:, sl], k[:, sl], v[:, sl], g[:, sl], beta[:, sl], s, chunk)
        outs.append(o_seg)
        start += seg_len
    o = jnp.concatenate(outs, axis=1) if len(outs) > 1 else outs[0]
    o = rms_norm(o, gdn_norm_w) * jax.nn.silu(z.reshape(b, t, H, DH).astype(f32))
    y_gdn = o.reshape(b, t, G).astype(x.dtype)

    xc, conv_lru_new = causal_conv(lru_pre, conv_lru, lru_conv_w)
    xc = (xc + lru_conv_b.astype(x.dtype)).astype(f32)
    xb = xc.reshape(b, t, LRU_BLOCKS, LRU_BLOCK_DIM)
    r = jax.nn.sigmoid(jnp.einsum('btni,nij->btnj', xb, lru_wa.astype(f32)).reshape(b, t, L) + lru_ba.astype(f32))
    i = jax.nn.sigmoid(jnp.einsum('btni,nij->btnj', xb, lru_wx.astype(f32)).reshape(b, t, L) + lru_bx.astype(f32))
    log_a = -LRU_C * r * jax.nn.softplus(-lru_lambda.astype(f32))
    a = jnp.exp(log_a)
    mult = jnp.sqrt(-jnp.expm1(2.0 * log_a))
    if reset_first:
        mult = mult.at[:, 0].set(1.0)
    bterm = xc * i * mult
    bterm = bterm.at[:, 0].add(a[:, 0] * h_lru.astype(f32))

    def combine(e1, e2):
        a1, b1 = e1
        a2, b2 = e2
        return a1 * a2, a2 * b1 + b2

    _, h = lax.associative_scan(combine, (a, bterm), axis=1)
    h_last = h[:, -1]
    y_lru = rms_norm(h * jax.nn.gelu(gate_pre.astype(f32)), lru_norm_w).astype(x.dtype)

    mix = jnp.concatenate([y_gdn, y_lru], axis=-1)
    out = jnp.einsum('btc,cd->btd', mix, w_out)
    return out, (s, conv_gdn_new, h_last, conv_lru_new)


def moe(x, router_w, router_bias, w_gate, w_up, w_down, sh_gate, sh_up, sh_down):
    t, d = x.shape
    f32 = jnp.float32
    scores = jax.nn.sigmoid(jnp.einsum('td,de->te', x, router_w).astype(f32))
    sel = scores + router_bias.astype(f32)
    per_group = N_EXPERTS // N_GROUPS
    grp_score = lax.top_k(sel.reshape(t, N_GROUPS, per_group), 2)[0].sum(-1)
    _, gidx = lax.top_k(grp_score, TOPK_GROUPS)
    gmask = jax.nn.one_hot(gidx, N_GROUPS, dtype=f32).sum(1) > 0
    sel = jnp.where(jnp.repeat(gmask, per_group, axis=1), sel, -jnp.inf)
    _, eidx = lax.top_k(sel, TOP_K)
    gate = jnp.take_along_axis(scores, eidx, axis=1)
    gate = gate / jnp.sum(gate, axis=-1, keepdims=True) * ROUTED_SCALE

    p = t * TOP_K
    e_flat = eidx.reshape(p)
    tok_flat = jnp.repeat(jnp.arange(t, dtype=jnp.int32), TOP_K)
    gate_flat = gate.reshape(p)
    order = jnp.argsort(e_flat)
    s_e, s_tok, s_gate = e_flat[order], tok_flat[order], gate_flat[order]
    counts = jnp.bincount(e_flat, length=N_EXPERTS)
    start = jnp.cumsum(counts) - counts
    pcounts = (counts + MOE_BLOCK - 1) // MOE_BLOCK * MOE_BLOCK
    pstart = jnp.cumsum(pcounts) - pcounts
    dest = pstart[s_e] + (jnp.arange(p) - start[s_e])
    n_blocks = (p + N_EXPERTS * (MOE_BLOCK - 1) + MOE_BLOCK - 1) // MOE_BLOCK
    rows = n_blocks * MOE_BLOCK
    row_tok = jnp.full((rows,), t, jnp.int32).at[dest].set(s_tok)
    row_gate = jnp.zeros((rows,), f32).at[dest].set(s_gate)
    blk_exp = jnp.minimum(jnp.searchsorted(jnp.cumsum(pcounts), jnp.arange(n_blocks) * MOE_BLOCK, side='right'),
                          N_EXPERTS - 1)
    x_pad = jnp.concatenate([x, jnp.zeros((1, d), x.dtype)], axis=0)

    def run_block(args):
        tok_b, gate_b, e_b = args
        xb = x_pad[tok_b]
        hb = jax.nn.silu(xb @ w_gate[e_b]) * (xb @ w_up[e_b])
        return (hb @ w_down[e_b]).astype(f32) * gate_b[:, None]

    y_rows = lax.map(run_block, (row_tok.reshape(n_blocks, MOE_BLOCK), row_gate.reshape(n_blocks, MOE_BLOCK), blk_exp))
    routed = jax.ops.segment_sum(y_rows.reshape(rows, d), row_tok, num_segments=t + 1)[:t]
    shared = (jax.nn.silu(x @ sh_gate) * (x @ sh_up)) @ sh_down
    return (routed + shared.astype(f32)).astype(x.dtype)


def setup_inputs(seed: int = 0) -> dict:
    key = jax.random.key(seed)
    ks = jax.random.split(key, 40)
    f32 = jnp.float32

    def nrm(k, shape, scale):
        return scale * jax.random.normal(k, shape, f32)

    G, H, DH, L, E = GDN_WIDTH, GDN_HEADS, GDN_HEAD_DIM, LRU_WIDTH, N_EXPERTS
    dt = jnp.exp(jax.random.uniform(ks[10], (DEPTH, H), f32, np.log(1e-3), np.log(1e-1)))
    u = jax.random.uniform(ks[18], (DEPTH, L), f32, 0.9, 0.999)
    sig = u ** (1.0 / LRU_C)
    return {
        'x_prompt': nrm(ks[0], (BATCH, SEQ, D_MODEL), 1.0),
        'x_sample': nrm(ks[1], (DEC_BATCH, DEC_SEQ, D_MODEL), 1.0),
        'state_gdn': nrm(ks[2], (DEPTH, DEC_BATCH, H, DH, DH), 0.1),
        'state_gdn_conv': nrm(ks[3], (DEPTH, DEC_BATCH, CONV_W - 1, 3 * G), 1.0),
        'state_lru': nrm(ks[4], (DEPTH, DEC_BATCH, L), 0.5),
        'state_lru_conv': nrm(ks[5], (DEPTH, DEC_BATCH, CONV_W - 1, L), 1.0),
        'meta_tokens': nrm(ks[6], (N_META, D_MODEL), 1.0),
        'w_in': nrm(ks[7], (DEPTH, D_MODEL, IN_COLS), D_MODEL ** -0.5),
        'gdn_conv_w': nrm(ks[8], (DEPTH, CONV_W, 3 * G), CONV_W ** -0.5),
        'gdn_a_log': jnp.log(jax.random.uniform(ks[9], (DEPTH, H), f32, 1.0, 16.0)),
        'gdn_dt_bias': dt + jnp.log(-jnp.expm1(-dt)),
        'gdn_norm_w': 1.0 + nrm(ks[11], (DEPTH, DH), 0.05),
        'lru_conv_w': nrm(ks[12], (DEPTH, CONV_W, L), CONV_W ** -0.5),
        'lru_conv_b': nrm(ks[13], (DEPTH, L), 0.02),
        'lru_gate_a_w': nrm(ks[14], (DEPTH, LRU_BLOCKS, LRU_BLOCK_DIM, LRU_BLOCK_DIM), LRU_BLOCK_DIM ** -0.5),
        'lru_gate_a_b': nrm(ks[15], (DEPTH, L), 0.02),
        'lru_gate_x_w': nrm(ks[16], (DEPTH, LRU_BLOCKS, LRU_BLOCK_DIM, LRU_BLOCK_DIM), LRU_BLOCK_DIM ** -0.5),
        'lru_gate_x_b': nrm(ks[17], (DEPTH, L), 0.02),
        'lru_lambda': jnp.log(sig) - jnp.log1p(-sig),
        'lru_norm_w': 1.0 + nrm(ks[19], (DEPTH, L), 0.05),
        'w_out': nrm(ks[20], (DEPTH, MIX_WIDTH, D_MODEL), MIX_WIDTH ** -0.5 * DEEPNORM_BETA),
        'ln_mix_g': 1.0 + nrm(ks[21], (DEPTH, D_MODEL), 0.05),
        'ln_mix_b': nrm(ks[22], (DEPTH, D_MODEL), 0.02),
        'router_w': nrm(ks[23], (DEPTH, D_MODEL, E), D_MODEL ** -0.5),
        'router_bias': nrm(ks[24], (DEPTH, E), 0.01),
        'expert_w_gate': nrm(ks[25], (DEPTH, E, D_MODEL, EXPERT_DIM), D_MODEL ** -0.5),
        'expert_w_up': nrm(ks[26], (DEPTH, E, D_MODEL, EXPERT_DIM), D_MODEL ** -0.5),
        'expert_w_down': nrm(ks[27], (DEPTH, E, EXPERT_DIM, D_MODEL), EXPERT_DIM ** -0.5 * DEEPNORM_BETA),
        'shared_w_gate': nrm(ks[28], (DEPTH, D_MODEL, SHARED_DIM), D_MODEL ** -0.5),
        'shared_w_up': nrm(ks[29], (DEPTH, D_MODEL, SHARED_DIM), D_MODEL ** -0.5),
        'shared_w_down': nrm(ks[30], (DEPTH, SHARED_DIM, D_MODEL), SHARED_DIM ** -0.5 * DEEPNORM_BETA),
        'ln_ffn_g': 1.0 + nrm(ks[31], (DEPTH, D_MODEL), 0.05),
        'ln_ffn_b': nrm(ks[32], (DEPTH, D_MODEL), 0.02),
    }


def reference(x_prompt, x_sample, state_gdn, state_gdn_conv, state_lru, state_lru_conv, meta_tokens,
              w_in, gdn_conv_w, gdn_a_log, gdn_dt_bias, gdn_norm_w, lru_conv_w, lru_conv_b,
              lru_gate_a_w, lru_gate_a_b, lru_gate_x_w, lru_gate_x_b, lru_lambda, lru_norm_w, w_out,
              ln_mix_g, ln_mix_b, router_w, router_bias, expert_w_gate, expert_w_up, expert_w_down,
              shared_w_gate, shared_w_up, shared_w_down, ln_ffn_g, ln_ffn_b):
    f32 = jnp.float32
    dt = x_prompt.dtype
    b, seq = x_prompt.shape[0], x_prompt.shape[1]
    ts = x_sample.shape[1]
    xp = jnp.concatenate([jnp.broadcast_to(meta_tokens.astype(dt)[None], (b, N_META, D_MODEL)), x_prompt], axis=1)
    xs = x_sample
    n_prompt_rows = b * (N_META + seq)
    seg_p = ((N_META, N_META), (seq, GDN_CHUNK))
    seg_s = ((ts, min(GDN_CHUNK, ts)),)
    zeros_p = (jnp.zeros((b, GDN_HEADS, GDN_HEAD_DIM, GDN_HEAD_DIM), f32),
               jnp.zeros((b, CONV_W - 1, 3 * GDN_WIDTH), dt),
               jnp.zeros((b, LRU_WIDTH), f32),
               jnp.zeros((b, CONV_W - 1, LRU_WIDTH), dt))
    st_p_all, st_s_all = [], []
    for l in range(DEPTH):
        lw = (w_in[l], gdn_conv_w[l], gdn_a_log[l], gdn_dt_bias[l], gdn_norm_w[l], lru_conv_w[l], lru_conv_b[l],
              lru_gate_a_w[l], lru_gate_a_b[l], lru_gate_x_w[l], lru_gate_x_b[l], lru_lambda[l], lru_norm_w[l], w_out[l])
        mp, st_p = mixer(xp, *zeros_p, *lw, segments=seg_p, reset_first=True)
        ms, st_s = mixer(xs, state_gdn[l], state_gdn_conv[l], state_lru[l], state_lru_conv[l], *lw,
                         segments=seg_s, reset_first=False)
        st_p_all.append(st_p)
        st_s_all.append(st_s)
        xp = layer_norm(DEEPNORM_ALPHA * xp + mp, ln_mix_g[l], ln_mix_b[l])
        xs = layer_norm(DEEPNORM_ALPHA * xs + ms, ln_mix_g[l], ln_mix_b[l])
        flat = jnp.concatenate([xp.reshape(-1, D_MODEL), xs.reshape(-1, D_MODEL)], axis=0)
        f = moe(flat, router_w[l], router_bias[l], expert_w_gate[l], expert_w_up[l], expert_w_down[l],
                shared_w_gate[l], shared_w_up[l], shared_w_down[l])
        xp = layer_norm(DEEPNORM_ALPHA * xp + f[:n_prompt_rows].reshape(xp.shape), ln_ffn_g[l], ln_ffn_b[l])
        xs = layer_norm(DEEPNORM_ALPHA * xs + f[n_prompt_rows:].reshape(xs.shape), ln_ffn_g[l], ln_ffn_b[l])
    new_gdn_p = jnp.stack([st[0] for st in st_p_all], axis=0)
    new_gdn_conv_p = jnp.stack([st[1] for st in st_p_all], axis=0)
    new_lru_p = jnp.stack([st[2] for st in st_p_all], axis=0)
    new_lru_conv_p = jnp.stack([st[3] for st in st_p_all], axis=0)
    new_gdn_s = jnp.stack([st[0] for st in st_s_all], axis=0)
    new_gdn_conv_s = jnp.stack([st[1] for st in st_s_all], axis=0)
    new_lru_s = jnp.stack([st[2] for st in st_s_all], axis=0)
    new_lru_conv_s = jnp.stack([st[3] for st in st_s_all], axis=0)
    y_prompt = xp[:, N_META:]
    return (y_prompt, xs, new_gdn_p, new_gdn_conv_p, new_lru_p, new_lru_conv_p,
            new_gdn_s, new_gdn_conv_s, new_lru_s, new_lru_conv_s)
```

```python
import functools

import jax
import jax.numpy as jnp
from jax import lax
from jax.experimental import pallas as pl
from jax.experimental.pallas import tpu as pltpu

F32 = jnp.float32
BF16 = jnp.bfloat16
I32 = jnp.int32
U32 = jnp.uint32

LANES = 128
SUBLANES = 8
VMEM_LIMIT = 56 * 1024 * 1024

D_MODEL = 1024
N_META = 16
GDN_WIDTH = 512
GDN_HEADS = 4
HEAD_DIM = 128
LRU_WIDTH = 512
CONV_W = 4
LRU_C = 8.0
N_EXPERTS = 256
TOP_K = 8
N_GROUPS = 8
TOPK_GROUPS = 4
GROUP_SIZE = N_EXPERTS // N_GROUPS
EXPERT_DIM = 256
ROUTED_SCALE = 2.5
MOE_BLOCK = 128
DEEPNORM_ALPHA = 2.0 ** 0.25
LN_EPS = 1e-5
RMS_EPS = 1e-6

QKV = 3 * GDN_WIDTH
OFF_Z = QKV
OFF_LRU = OFF_Z + GDN_WIDTH
OFF_GATE = OFF_LRU + LRU_WIDTH
OFF_BA = OFF_GATE + LRU_WIDTH
PROJ_COLS = OFF_BA + LANES

GDN_CHUNK = 64
HIST = CONV_W - 1
HIST_ROW = SUBLANES - HIST


def _cparams(n_axes):
    return pltpu.CompilerParams(dimension_semantics=("arbitrary",) * n_axes, vmem_limit_bytes=VMEM_LIMIT)


def _full_spec(shape):
    nd = len(shape)
    return pl.BlockSpec(shape, lambda *_: (0,) * nd)


def _mm(a, b):
    return jnp.dot(a.astype(BF16), b.astype(BF16), preferred_element_type=F32)


def _mm_nt(a, b):
    return lax.dot_general(a.astype(BF16), b.astype(BF16), (((1,), (1,)), ((), ())), preferred_element_type=F32)


def _split3(x):
    hi = x.astype(BF16)
    r = x - hi.astype(F32)
    mid = r.astype(BF16)
    lo = (r - mid.astype(F32)).astype(BF16)
    return hi, mid, lo


def _mm_exact_lhs(m01, x):
    mb = m01.astype(BF16)
    hi, mid, lo = _split3(x)
    dot = functools.partial(jnp.dot, preferred_element_type=F32)
    return dot(mb, hi) + dot(mb, mid) + dot(mb, lo)


def _mm_3pass(a, b):
    a_hi = a.astype(BF16)
    a_lo = (a - a_hi.astype(F32)).astype(BF16)
    b_hi = b.astype(BF16)
    b_lo = (b - b_hi.astype(F32)).astype(BF16)
    dot = functools.partial(jnp.dot, preferred_element_type=F32)
    return dot(a_hi, b_hi) + dot(a_lo, b_hi) + dot(a_hi, b_lo)


def _softplus(x):
    return jnp.maximum(x, 0.0) + jnp.log1p(jnp.exp(-jnp.abs(x)))


def _sigmoid(x):
    return 1.0 / (1.0 + jnp.exp(-x))


def _silu(x):
    return x * _sigmoid(x)


def _layer_norm(x, g, b):
    mu = jnp.mean(x, axis=-1, keepdims=True)
    xc = x - mu
    var = jnp.mean(xc * xc, axis=-1, keepdims=True)
    return xc * lax.rsqrt(var + LN_EPS) * g + b


def _proj_kernel(x_ref, w_ref, o_ref):
    o_ref[...] = jnp.dot(x_ref[...].astype(BF16), w_ref[...], preferred_element_type=F32)


def _proj(x, w):
    rows = x.shape[0]
    tm = 256
    return pl.pallas_call(
        _proj_kernel,
        out_shape=jax.ShapeDtypeStruct((rows, PROJ_COLS), F32),
        grid=(rows // tm,),
        in_specs=[pl.BlockSpec((tm, D_MODEL), lambda i: (i, 0)), _full_spec((D_MODEL, PROJ_COLS))],
        out_specs=pl.BlockSpec((tm, PROJ_COLS), lambda i: (i, 0)),
        compiler_params=_cparams(1),
        name="in_proj",
    )(x, w)


def _scan8(a, b):
    row = lax.broadcasted_iota(I32, a.shape, 0)
    for s in (1, 2, 4):
        keep = row >= s
        a_prev = jnp.where(keep, pltpu.roll(a, s, 0), 1.0)
        b_prev = jnp.where(keep, pltpu.roll(b, s, 0), 0.0)
        b = a * b_prev + b
        a = a * a_prev
    return a, b


def _conv_cols(ext_ref, w_ref, c0, width, tt):
    acc = None
    for j in range(CONV_W):
        term = ext_ref[HIST_ROW + j:HIST_ROW + j + tt, c0:c0 + width] * w_ref[j:j + 1, c0:c0 + width]
        acc = term if acc is None else acc + term
    return acc


def _mixer_kernel(proj_ref, s0_ref, cg0_ref, h0_ref, cl0_ref,
                  gcw_ref, lcw_ref, lcb_ref, wa_ref, wx_ref, ba_ref, bx_ref, lam_ref, lnw_ref, gnw_ref,
                  alog_ref, dtb_ref,
                  y_ref, s_out_ref, cg_out_ref, h_out_ref, cl_out_ref,
                  s_scr, extg, extl, h_scr, q_scr, k_scr, v_scr, beta_scr, gc_scr, a_scr, b_scr, hs_scr,
                  *, tt, chunk, t_valid, reset_first):
    t = pl.program_id(1)
    t_last = (t_valid - 1) // tt
    off_last = t_valid - t_last * tt

    @pl.when(t == 0)
    def _init():
        s_scr[...] = s0_ref[...]
        extg[HIST_ROW:SUBLANES, :] = cg0_ref[...]
        extl[HIST_ROW:SUBLANES, :] = cl0_ref[...]
        h_scr[...] = h0_ref[...]

    extg[SUBLANES:SUBLANES + tt, :] = proj_ref[:, 0:QKV]
    extl[SUBLANES:SUBLANES + tt, :] = proj_ref[:, OFF_LRU:OFF_LRU + LRU_WIDTH]

    @pl.when(t == t_last)
    def _conv_state():
        cg_out_ref[...] = extg[SUBLANES + off_last - HIST:SUBLANES + off_last, :]
        cl_out_ref[...] = extl[SUBLANES + off_last - HIST:SUBLANES + off_last, :]

    row_in_tile = lax.broadcasted_iota(I32, (tt, LANES), 0)
    row_global = row_in_tile + t * tt

    for hd in range(GDN_HEADS):
        for part, scr in enumerate((q_scr, k_scr, v_scr)):
            c0 = part * GDN_WIDTH + hd * HEAD_DIM
            x = _silu(_conv_cols(extg, gcw_ref, c0, HEAD_DIM, tt))
            if part < 2:
                x = x * lax.rsqrt(jnp.sum(x * x, axis=-1, keepdims=True) + RMS_EPS)
            if part == 0:
                x = x * (HEAD_DIM ** -0.5)
            scr[:, hd * HEAD_DIM:(hd + 1) * HEAD_DIM] = x

    ba = proj_ref[:, OFF_BA:OFF_BA + LANES]
    valid = row_global < t_valid
    beta_scr[...] = jnp.where(valid, _sigmoid(ba), 0.0)
    g = jnp.where(valid, -jnp.exp(alog_ref[...]) * _softplus(ba + dtb_ref[...]), 0.0)
    if chunk == SUBLANES:
        _, gc = _scan8(jnp.ones_like(g), g)
    else:
        ri = lax.broadcasted_iota(I32, (tt, tt), 0)
        ci = lax.broadcasted_iota(I32, (tt, tt), 1)
        seg = jnp.where((ci <= ri) & ((ri // chunk) == (ci // chunk)), 1.0, 0.0)
        gc = _mm_exact_lhs(seg, g)
    gc_scr[...] = gc

    ri = lax.broadcasted_iota(I32, (chunk, chunk), 0)
    ci = lax.broadcasted_iota(I32, (chunk, chunk), 1)
    causal = ri >= ci
    strict = ri > ci
    eye = jnp.where(ri == ci, 1.0, 0.0)
    n_levels = chunk.bit_length() - 1
    gnw = gnw_ref[...]

    def chunk_body(c, carry):
        r0 = pl.multiple_of(c * chunk, chunk)
        rows = pl.ds(r0, chunk)
        beta_c = beta_scr[rows, :]
        gc_c = gc_scr[rows, :]
        gc_t = gc_c.T
        for hd in range(GDN_HEADS):
            cols = slice(hd * HEAD_DIM, (hd + 1) * HEAD_DIM)
            q = q_scr[rows, cols]
            k = k_scr[rows, cols]
            v = v_scr[rows, cols]
            bcol = beta_c[:, hd:hd + 1]
            gcol = gc_c[:, GDN_HEADS + hd:GDN_HEADS + hd + 1]
            grow = gc_t[GDN_HEADS + hd:GDN_HEADS + hd + 1, :]
            glast = gc_c[chunk - 1:chunk, GDN_HEADS + hd:GDN_HEADS + hd + 1]
            decay = jnp.exp(jnp.where(causal, gcol - grow, -jnp.inf))
            kk = _mm_nt(k, k)
            qk = _mm_nt(q, k) * decay
            neg_a = jnp.where(strict, -(bcol * kk * decay), 0.0)
            inv = eye + neg_a
            power = neg_a
            for _ in range(n_levels - 1):
                power = _mm(power, power)
                inv = inv + _mm(inv, power)
            eg = jnp.exp(gcol)
            rhs = jnp.concatenate([bcol * v, (bcol * eg) * k], axis=1)
            sol = _mm(inv, rhs)
            tv = sol[:, :HEAD_DIM]
            tk = sol[:, HEAD_DIM:]
            s = s_scr[hd]
            u = tv - _mm(tk, s)
            o = _mm(q * eg, s) + _mm(qk, u)
            k_dec = k * jnp.exp(glast - gcol)
            s_scr[hd] = jnp.exp(glast) * s + _mm(k_dec.T, u)
            z = proj_ref[rows, OFF_Z + hd * HEAD_DIM:OFF_Z + (hd + 1) * HEAD_DIM]
            on = o * lax.rsqrt(jnp.mean(o * o, axis=-1, keepdims=True) + RMS_EPS) * gnw
            y_ref[rows, cols] = on * _silu(z)
        return carry

    lax.fori_loop(0, tt // chunk, chunk_body, 0)

    @pl.when(t == t_last)
    def _gdn_state():
        s_out_ref[...] = s_scr[...]

    sp_lam = _softplus(-lam_ref[...])
    for p in range(LRU_WIDTH // LANES):
        cols = slice(p * LANES, (p + 1) * LANES)
        xc = _conv_cols(extl, lcw_ref, p * LANES, LANES, tt) + lcb_ref[:, cols]
        r = _sigmoid(_mm_3pass(xc, wa_ref[p]) + ba_ref[:, cols])
        i = _sigmoid(_mm_3pass(xc, wx_ref[p]) + bx_ref[:, cols])
        log_a = -LRU_C * r * sp_lam[:, cols]
        a = jnp.exp(log_a)
        mult = jnp.sqrt(-jnp.tanh(log_a) * (a * a + 1.0))
        if reset_first:
            mult = jnp.where(row_global == 0, 1.0, mult)
        a_scr[:, cols] = a
        b_scr[:, cols] = xc * i * mult

    def group_body(gi, h):
        rows = pl.ds(pl.multiple_of(gi * SUBLANES, SUBLANES), SUBLANES)
        a_cum, b_cum = _scan8(a_scr[rows, :], b_scr[rows, :])
        hs = a_cum * h + b_cum
        hs_scr[rows, :] = hs
        return hs[SUBLANES - 1:SUBLANES, :]

    h_scr[...] = lax.fori_loop(0, tt // SUBLANES, group_body, h_scr[...])

    @pl.when(t == t_last)
    def _lru_state():
        h_out_ref[...] = hs_scr[off_last - 1:off_last, :]

    gate = proj_ref[:, OFF_GATE:OFF_GATE + LRU_WIDTH]
    hg = hs_scr[...] * jax.nn.gelu(gate, approximate=True)
    y_ref[:, GDN_WIDTH:] = hg * lax.rsqrt(jnp.mean(hg * hg, axis=-1, keepdims=True) + RMS_EPS) * lnw_ref[...]

    new_g = extg[tt + HIST_ROW:tt + SUBLANES, :]
    new_l = extl[tt + HIST_ROW:tt + SUBLANES, :]
    extg[HIST_ROW:SUBLANES, :] = new_g
    extl[HIST_ROW:SUBLANES, :] = new_l


def _mixer(proj, states, wts, *, batch, seq, tt, chunk, t_valid, reset_first, row0):
    n_t = seq // tt
    blk0 = row0 // tt
    s0, cg0, h0, cl0 = states
    kern = functools.partial(_mixer_kernel, tt=tt, chunk=chunk, t_valid=t_valid, reset_first=reset_first)
    state_specs = [
        pl.BlockSpec((None, GDN_HEADS, HEAD_DIM, HEAD_DIM), lambda b, t: (b, 0, 0, 0)),
        pl.BlockSpec((None, HIST, QKV), lambda b, t: (b, 0, 0)),
        pl.BlockSpec((None, 1, LRU_WIDTH), lambda b, t: (b, 0, 0)),
        pl.BlockSpec((None, HIST, LRU_WIDTH), lambda b, t: (b, 0, 0)),
    ]
    in_specs = ([pl.BlockSpec((tt, PROJ_COLS), lambda b, t: (blk0 + b * n_t + t, 0))] + state_specs
                + [_full_spec(w.shape) for w in wts])
    out_shape = [
        jax.ShapeDtypeStruct((batch * seq, D_MODEL), F32),
        jax.ShapeDtypeStruct((batch, GDN_HEADS, HEAD_DIM, HEAD_DIM), F32),
        jax.ShapeDtypeStruct((batch, HIST, QKV), F32),
        jax.ShapeDtypeStruct((batch, 1, LRU_WIDTH), F32),
        jax.ShapeDtypeStruct((batch, HIST, LRU_WIDTH), F32),
    ]
    out_specs = [pl.BlockSpec((tt, D_MODEL), lambda b, t: (b * n_t + t, 0))] + state_specs
    scratch = [
        pltpu.VMEM((GDN_HEADS, HEAD_DIM, HEAD_DIM), F32),
        pltpu.VMEM((tt + SUBLANES, QKV), F32),
        pltpu.VMEM((tt + SUBLANES, LRU_WIDTH), F32),
        pltpu.VMEM((1, LRU_WIDTH), F32),
        pltpu.VMEM((tt, GDN_WIDTH), F32),
        pltpu.VMEM((tt, GDN_WIDTH), F32),
        pltpu.VMEM((tt, GDN_WIDTH), F32),
        pltpu.VMEM((tt, LANES), F32),
        pltpu.VMEM((tt, LANES), F32),
        pltpu.VMEM((tt, LRU_WIDTH), F32),
        pltpu.VMEM((tt, LRU_WIDTH), F32),
        pltpu.VMEM((tt, LRU_WIDTH), F32),
    ]
    return pl.pallas_call(
        kern,
        out_shape=out_shape,
        grid=(batch, n_t),
        in_specs=in_specs,
        out_specs=out_specs,
        scratch_shapes=scratch,
        compiler_params=_cparams(2),
        name="mixer_prompt" if reset_first else "mixer_sample",
    )(proj, s0, cg0, h0, cl0, *wts)


def _route(scores, sel):
    tm = sel.shape[1]
    neg = -jnp.inf
    in_grp = lax.broadcasted_iota(I32, (GROUP_SIZE, tm), 0).astype(F32)
    grp_rows = []
    for gi in range(N_GROUPS):
        blk = sel[gi * GROUP_SIZE:(gi + 1) * GROUP_SIZE, :]
        m1 = jnp.max(blk, axis=0, keepdims=True)
        first = jnp.min(jnp.where(blk == m1, in_grp, float(GROUP_SIZE)), axis=0, keepdims=True)
        m2 = jnp.max(jnp.where(in_grp == first, neg, blk), axis=0, keepdims=True)
        grp_rows.append(m1 + m2)
    grp = jnp.concatenate(grp_rows, axis=0)
    g_iota = lax.broadcasted_iota(I32, (N_GROUPS, tm), 0).astype(F32)
    gmask = jnp.zeros((N_GROUPS, tm), F32)
    for _ in range(TOPK_GROUPS):
        m = jnp.max(grp, axis=0, keepdims=True)
        idx = jnp.min(jnp.where(grp == m, g_iota, float(N_GROUPS)), axis=0, keepdims=True)
        hit = g_iota == idx
        gmask = jnp.where(hit, 1.0, gmask)
        grp = jnp.where(hit, neg, grp)
    gfull = jnp.concatenate(
        [jnp.broadcast_to(gmask[gi:gi + 1, :], (GROUP_SIZE, tm)) for gi in range(N_GROUPS)], axis=0)
    cur = jnp.where(gfull > 0.0, sel, neg)
    e_iota = lax.broadcasted_iota(I32, (N_EXPERTS, tm), 0).astype(F32)
    chosen = jnp.zeros((N_EXPERTS, tm), F32)
    idxs, gates = [], []
    for _ in range(TOP_K):
        m = jnp.max(cur, axis=0, keepdims=True)
        idx = jnp.min(jnp.where(cur == m, e_iota, float(N_EXPERTS)), axis=0, keepdims=True)
        hit = e_iota == idx
        gates.append(jnp.sum(jnp.where(hit, scores, 0.0), axis=0, keepdims=True))
        idxs.append(idx)
        cur = jnp.where(hit, neg, cur)
        chosen = jnp.where(hit, 1.0, chosen)
    return jnp.concatenate(idxs, axis=0), jnp.concatenate(gates, axis=0), chosen, e_iota


def _post_mixer_kernel(yp_ref, ys_ref, x_ref, wo_ref, g_ref, b_ref, rw_ref, rb_ref,
                       x1_ref, x1p_ref, eidx_ref, gate_ref, rank_ref, cnt_ref, cnt_scr, *, n_prompt_tiles):
    i = pl.program_id(0)

    @pl.when(i == 0)
    def _init():
        cnt_scr[...] = jnp.zeros_like(cnt_scr)

    y = jnp.where(i < n_prompt_tiles, yp_ref[...], ys_ref[...])
    mixed = jnp.dot(y.astype(BF16), wo_ref[...], preferred_element_type=F32)
    x1 = _layer_norm(DEEPNORM_ALPHA * x_ref[...] + mixed, g_ref[...], b_ref[...])
    x1_ref[...] = x1
    x1b = x1.astype(BF16)
    half = D_MODEL // 2
    xb = x1b.astype(F32)
    lo = lax.bitcast_convert_type(xb[:, :half], U32) >> 16
    hi = lax.bitcast_convert_type(xb[:, half:], U32) & jnp.uint32(0xFFFF0000)
    x1p_ref[...] = hi | lo

    logits = lax.dot_general(rw_ref[...], x1b, (((1,), (1,)), ((), ())), preferred_element_type=F32)
    scores = _sigmoid(logits)
    idxs, gates, chosen, e_iota = _route(scores, scores + rb_ref[...])
    gates = gates / jnp.sum(gates, axis=0, keepdims=True) * ROUTED_SCALE

    tm = scores.shape[1]
    ri = lax.broadcasted_iota(I32, (tm, tm), 0)
    ci = lax.broadcasted_iota(I32, (tm, tm), 1)
    before = jnp.where(ri < ci, 1.0, 0.0).astype(BF16)
    rank_all = jnp.dot(chosen.astype(BF16), before, preferred_element_type=F32) + cnt_scr[...]
    ranks = [jnp.sum(jnp.where(e_iota == idxs[j:j + 1, :], rank_all, 0.0), axis=0, keepdims=True)
             for j in range(TOP_K)]
    eidx_ref[...] = idxs.astype(I32)
    gate_ref[...] = gates
    rank_ref[...] = jnp.concatenate(ranks, axis=0).astype(I32)
    cnt_scr[...] = cnt_scr[...] + jnp.sum(chosen, axis=1, keepdims=True)
    cnt_ref[...] = cnt_scr[...]


def _post_mixer(y_prompt, y_sample, x, w_out, ln_g, ln_b, router_wt, router_b):
    rows = x.shape[0]
    tm = 256
    n_p = y_prompt.shape[0] // tm
    assert y_prompt.shape[0] % tm == 0 and y_sample.shape[0] % tm == 0
    row_spec = lambda w: pl.BlockSpec((tm, w), lambda i: (i, 0))
    slot_spec = pl.BlockSpec((TOP_K, tm), lambda i: (0, i))
    yp_spec = pl.BlockSpec((tm, D_MODEL), lambda i: (jnp.minimum(i, n_p - 1), 0))
    ys_spec = pl.BlockSpec((tm, D_MODEL), lambda i: (jnp.maximum(i - n_p, 0), 0))
    return pl.pallas_call(
        functools.partial(_post_mixer_kernel, n_prompt_tiles=n_p),
        out_shape=[
            jax.ShapeDtypeStruct((rows, D_MODEL), F32),
            jax.ShapeDtypeStruct((rows, D_MODEL // 2), U32),
            jax.ShapeDtypeStruct((TOP_K, rows), I32),
            jax.ShapeDtypeStruct((TOP_K, rows), F32),
            jax.ShapeDtypeStruct((TOP_K, rows), I32),
            jax.ShapeDtypeStruct((N_EXPERTS, 1), F32),
        ],
        grid=(rows // tm,),
        in_specs=[yp_spec, ys_spec, row_spec(D_MODEL), _full_spec(w_out.shape), _full_spec(ln_g.shape),
                  _full_spec(ln_b.shape), _full_spec(router_wt.shape), _full_spec(router_b.shape)],
        out_specs=[row_spec(D_MODEL), row_spec(D_MODEL // 2), slot_spec, slot_spec, slot_spec,
                   _full_spec((N_EXPERTS, 1))],
        scratch_shapes=[pltpu.VMEM((N_EXPERTS, 1), F32)],
        compiler_params=_cparams(1),
        name="post_mixer_router",
    )(y_prompt, y_sample, x, w_out, ln_g, ln_b, router_wt, router_b)


def _plan_kernel(cnt_ref, eidx_ref, rank_ref, dest_ref, blk_ref, nblk_ref, start_scr, *, n_blk_pad):
    i = pl.program_id(0)

    @pl.when(i == 0)
    def _offsets():
        cnt = cnt_ref[...].astype(I32)
        nblk = (cnt + (MOE_BLOCK - 1)) >> (MOE_BLOCK.bit_length() - 1)
        ri = lax.broadcasted_iota(I32, (N_EXPERTS, N_EXPERTS), 0)
        ci = lax.broadcasted_iota(I32, (N_EXPERTS, N_EXPERTS), 1)
        lower = jnp.where(ci <= ri, 1.0, 0.0).astype(BF16)
        lo = jnp.broadcast_to((nblk & 63).astype(F32), (N_EXPERTS, LANES)).astype(BF16)
        hi = jnp.broadcast_to((nblk >> 6).astype(F32), (N_EXPERTS, LANES)).astype(BF16)
        end_blk = (jnp.dot(lower, hi, preferred_element_type=F32) * 64.0
                   + jnp.dot(lower, lo, preferred_element_type=F32))
        start_scr[...] = (end_blk - jnp.broadcast_to(nblk.astype(F32), (N_EXPERTS, LANES))) * float(MOE_BLOCK)
        blk_id = lax.broadcasted_iota(I32, (N_EXPERTS, n_blk_pad), 1).astype(F32)
        done = jnp.where(end_blk[:, 0:1] <= blk_id, 1.0, 0.0)
        blk_ref[...] = jnp.minimum(jnp.sum(done, axis=0, keepdims=True), float(N_EXPERTS - 1)).astype(I32)
        nblk_ref[...] = end_blk[N_EXPERTS - 1:N_EXPERTS, 0:1].astype(I32)

    start = start_scr[:, 0:1]
    eidx = eidx_ref[...]
    tm = eidx.shape[1]
    e_iota = lax.broadcasted_iota(I32, (N_EXPERTS, tm), 0)
    rows = [jnp.sum(jnp.where(e_iota == eidx[j:j + 1, :], start, 0.0), axis=0, keepdims=True)
            for j in range(TOP_K)]
    dest_ref[...] = jnp.concatenate(rows, axis=0).astype(I32) + rank_ref[...]


def _plan(counts, eidx, rank, n_blk_pad):
    rows = eidx.shape[1]
    tm = MOE_BLOCK
    slot_spec = pl.BlockSpec((TOP_K, tm), lambda i: (0, i))
    return pl.pallas_call(
        functools.partial(_plan_kernel, n_blk_pad=n_blk_pad),
        out_shape=[
            jax.ShapeDtypeStruct((rows // tm, TOP_K, tm), I32),
            jax.ShapeDtypeStruct((1, n_blk_pad), I32),
            jax.ShapeDtypeStruct((1, 1), I32),
        ],
        grid=(rows // tm,),
        in_specs=[_full_spec((N_EXPERTS, 1)), slot_spec, slot_spec],
        out_specs=[pl.BlockSpec((None, TOP_K, tm), lambda i: (i, 0, 0)), _full_spec((1, n_blk_pad)),
                   _full_spec((1, 1))],
        scratch_shapes=[pltpu.VMEM((N_EXPERTS, LANES), F32)],
        compiler_params=_cparams(1),
        name="dispatch_plan",
    )(counts, eidx, rank)


def _dispatch_kernel(dest_ref, x_ref, buf_in_ref, xs_ref, dest_smem, sem_idx, sem_rows):
    del buf_in_ref
    tm = x_ref.shape[0]
    idx_copy = pltpu.make_async_copy(dest_ref.at[0], dest_smem, sem_idx)
    idx_copy.start()
    idx_copy.wait()

    def row_copy(tok, slot):
        return pltpu.make_async_copy(x_ref.at[pl.ds(tok, 1), :], xs_ref.at[pl.ds(dest_smem[slot, tok], 1), :],
                                     sem_rows)

    def start_body(tok, carry):
        for slot in range(TOP_K):
            row_copy(tok, slot).start()
        return carry

    def wait_body(tok, carry):
        for slot in range(TOP_K):
            row_copy(tok, slot).wait()
        return carry

    lax.fori_loop(0, tm, start_body, 0)
    lax.fori_loop(0, tm, wait_body, 0)


def _dispatch(dest, x1p, n_rows_sorted):
    rows, width = x1p.shape
    tm = MOE_BLOCK
    buf = jnp.zeros((n_rows_sorted, width), U32)
    return pl.pallas_call(
        _dispatch_kernel,
        out_shape=jax.ShapeDtypeStruct((n_rows_sorted, width), U32),
        grid=(rows // tm,),
        in_specs=[pl.BlockSpec((1, TOP_K, tm), lambda i: (i, 0, 0)),
                  pl.BlockSpec((tm, width), lambda i: (i, 0)),
                  pl.BlockSpec(memory_space=pl.ANY)],
        out_specs=pl.BlockSpec(memory_space=pl.ANY),
        scratch_shapes=[pltpu.SMEM((TOP_K, tm), I32), pltpu.SemaphoreType.DMA, pltpu.SemaphoreType.DMA],
        input_output_aliases={2: 0},
        compiler_params=_cparams(1),
        name="moe_dispatch",
    )(dest, x1p, buf)


def _experts_kernel(blk_ref, nblk_ref, xs_ref, wg_ref, wu_ref, wd_ref, ys_ref, wg_b, wu_b, wd_b):
    b = pl.program_id(0)
    prev = blk_ref[jnp.maximum(b - 1, 0)]

    @pl.when((b == 0) | (blk_ref[b] != prev))
    def _load_weights():
        wg_b[...] = wg_ref[...].astype(BF16)
        wu_b[...] = wu_ref[...].astype(BF16)
        wd_b[...] = wd_ref[...].astype(BF16)

    @pl.when(b < nblk_ref[0])
    def _ffn():
        half = D_MODEL // 2
        w = xs_ref[...]
        x_lo = lax.bitcast_convert_type(w << 16, F32).astype(BF16)
        x_hi = lax.bitcast_convert_type(w & jnp.uint32(0xFFFF0000), F32).astype(BF16)
        dot = functools.partial(jnp.dot, preferred_element_type=F32)
        gate = dot(x_lo, wg_b[:half, :]) + dot(x_hi, wg_b[half:, :])
        up = dot(x_lo, wu_b[:half, :]) + dot(x_hi, wu_b[half:, :])
        hidden = (_silu(gate) * up).astype(BF16)
        ys_ref[...] = dot(hidden, wd_b[...])

    @pl.when(b >= nblk_ref[0])
    def _unused():
        ys_ref[...] = jnp.zeros_like(ys_ref)


def _experts(blk_exp, n_blk_used, xs, w_gate, w_up, w_down):
    n_blk = xs.shape[0] // MOE_BLOCK
    row_map = lambda b, be, nb: (jnp.minimum(b, nb[0] - 1), 0)
    grid_spec = pltpu.PrefetchScalarGridSpec(
        num_scalar_prefetch=2,
        grid=(n_blk,),
        in_specs=[
            pl.BlockSpec((MOE_BLOCK, D_MODEL // 2), row_map),
            pl.BlockSpec((None, D_MODEL, EXPERT_DIM), lambda b, be, nb: (be[b], 0, 0)),
            pl.BlockSpec((None, D_MODEL, EXPERT_DIM), lambda b, be, nb: (be[b], 0, 0)),
            pl.BlockSpec((None, EXPERT_DIM, D_MODEL), lambda b, be, nb: (be[b], 0, 0)),
        ],
        out_specs=pl.BlockSpec((MOE_BLOCK, D_MODEL), lambda b, be, nb: (b, 0)),
        scratch_shapes=[pltpu.VMEM((D_MODEL, EXPERT_DIM), BF16), pltpu.VMEM((D_MODEL, EXPERT_DIM), BF16),
                        pltpu.VMEM((EXPERT_DIM, D_MODEL), BF16)],
    )
    return pl.pallas_call(
        _experts_kernel,
        out_shape=jax.ShapeDtypeStruct((xs.shape[0], D_MODEL), F32),
        grid_spec=grid_spec,
        compiler_params=_cparams(1),
        name="moe_experts",
    )(blk_exp, n_blk_used, xs, w_gate, w_up, w_down)


def _combine_kernel(dest_ref, gate_ref, x1_ref, sg_ref, su_ref, sd_ref, g_ref, b_ref, ys_ref,
                    out_ref, dest_smem, ybuf, sem_idx, sem_rows):
    tm = x1_ref.shape[0]
    idx_copy = pltpu.make_async_copy(dest_ref.at[0], dest_smem, sem_idx)
    idx_copy.start()
    idx_copy.wait()

    def row_copy(tok, slot):
        return pltpu.make_async_copy(ys_ref.at[pl.ds(dest_smem[slot, tok], 1), :],
                                     ybuf.at[slot, pl.ds(tok, 1), :], sem_rows)

    def start_body(tok, carry):
        for slot in range(TOP_K):
            row_copy(tok, slot).start()
        return carry

    def wait_body(tok, carry):
        for slot in range(TOP_K):
            row_copy(tok, slot).wait()
        return carry

    lax.fori_loop(0, tm, start_body, 0)

    x1 = x1_ref[...]
    x1b = x1.astype(BF16)
    dot = functools.partial(jnp.dot, preferred_element_type=F32)
    hidden = (_silu(dot(x1b, sg_ref[...])) * dot(x1b, su_ref[...])).astype(BF16)
    acc = dot(hidden, sd_ref[...])

    lax.fori_loop(0, tm, wait_body, 0)
    gates = gate_ref[...]
    routed = None
    for slot in range(TOP_K):
        term = gates[:, slot:slot + 1] * ybuf[slot]
        routed = term if routed is None else routed + term
    out_ref[...] = _layer_norm(DEEPNORM_ALPHA * x1 + (routed + acc), g_ref[...], b_ref[...])


def _combine(dest, gates, x1, sh_gate, sh_up, sh_down, ln_g, ln_b, ys):
    rows = x1.shape[0]
    tm = MOE_BLOCK
    return pl.pallas_call(
        _combine_kernel,
        out_shape=jax.ShapeDtypeStruct((rows, D_MODEL), F32),
        grid=(rows // tm,),
        in_specs=[pl.BlockSpec((1, TOP_K, tm), lambda i: (i, 0, 0)),
                  pl.BlockSpec((tm, TOP_K), lambda i: (i, 0)),
                  pl.BlockSpec((tm, D_MODEL), lambda i: (i, 0)),
                  _full_spec(sh_gate.shape), _full_spec(sh_up.shape), _full_spec(sh_down.shape),
                  _full_spec(ln_g.shape), _full_spec(ln_b.shape),
                  pl.BlockSpec(memory_space=pl.ANY)],
        out_specs=pl.BlockSpec((tm, D_MODEL), lambda i: (i, 0)),
        scratch_shapes=[pltpu.SMEM((TOP_K, tm), I32), pltpu.VMEM((TOP_K, tm, D_MODEL), F32),
                        pltpu.SemaphoreType.DMA, pltpu.SemaphoreType.DMA],
        compiler_params=_cparams(1),
        name="moe_combine",
    )(dest, gates, x1, sh_gate, sh_up, sh_down, ln_g, ln_b, ys)


def _pair_blockdiag(w):
    nb, bd, _ = w.shape
    z = jnp.zeros((bd, bd), w.dtype)
    pairs = [jnp.block([[w[2 * p], z], [z, w[2 * p + 1]]]) for p in range(nb // 2)]
    return jnp.stack(pairs, axis=0)


def _lane_row(vals, offset):
    return jnp.zeros((1, LANES), F32).at[0, offset:offset + vals.shape[0]].set(vals.astype(F32))


def kernel(x_prompt, x_sample, state_gdn, state_gdn_conv, state_lru, state_lru_conv, meta_tokens, w_in, gdn_conv_w, gdn_a_log, gdn_dt_bias, gdn_norm_w, lru_conv_w, lru_conv_b, lru_gate_a_w, lru_gate_a_b, lru_gate_x_w, lru_gate_x_b, lru_lambda, lru_norm_w, w_out, ln_mix_g, ln_mix_b, router_w, router_bias, expert_w_gate, expert_w_up, expert_w_down, shared_w_gate, shared_w_up, shared_w_down, ln_ffn_g, ln_ffn_b):
    depth = w_in.shape[0]
    assert depth == 1
    batch, seq, _ = x_prompt.shape
    dec_batch, dec_seq, _ = x_sample.shape
    tt_p = 3 * GDN_CHUNK
    t_valid = N_META + seq
    seq_pad = -(-t_valid // tt_p) * tt_p
    assert (t_valid - 1) // tt_p * tt_p + HIST <= t_valid, "conv state must sit inside one tile"
    n_prompt = batch * seq_pad
    n_sample = dec_batch * dec_seq
    rows = n_prompt + n_sample
    assert rows % 256 == 0 and n_prompt % dec_seq == 0 and dec_seq == SUBLANES

    meta = jnp.broadcast_to(meta_tokens.astype(F32)[None], (batch, N_META, D_MODEL))
    pad = jnp.zeros((batch, seq_pad - t_valid, D_MODEL), F32)
    x_all = jnp.concatenate([jnp.concatenate([meta, x_prompt, pad], axis=1).reshape(n_prompt, D_MODEL),
                             x_sample.reshape(n_sample, D_MODEL)], axis=0)
    l = 0
    w = w_in[l]
    o_ba = 4 * GDN_WIDTH
    o_lru = o_ba + 2 * GDN_HEADS
    w_cat = jnp.concatenate([w[:, :o_ba], w[:, o_lru:], w[:, o_ba:o_lru],
                             jnp.zeros((D_MODEL, LANES - 2 * GDN_HEADS), w.dtype)], axis=1).astype(BF16)
    mixer_wts = (
        gdn_conv_w[l], lru_conv_w[l], lru_conv_b[l][None], _pair_blockdiag(lru_gate_a_w[l]),
        _pair_blockdiag(lru_gate_x_w[l]), lru_gate_a_b[l][None], lru_gate_x_b[l][None], lru_lambda[l][None],
        lru_norm_w[l][None], gdn_norm_w[l][None], _lane_row(gdn_a_log[l], GDN_HEADS),
        _lane_row(gdn_dt_bias[l], GDN_HEADS),
    )

    proj = _proj(x_all, w_cat)

    zeros_p = (jnp.zeros((batch, GDN_HEADS, HEAD_DIM, HEAD_DIM), F32), jnp.zeros((batch, HIST, QKV), F32),
               jnp.zeros((batch, 1, LRU_WIDTH), F32), jnp.zeros((batch, HIST, LRU_WIDTH), F32))
    y_mix_p, gdn_p, gconv_p, lru_p, lconv_p = _mixer(
        proj, zeros_p, mixer_wts, batch=batch, seq=seq_pad, tt=tt_p, chunk=GDN_CHUNK, t_valid=t_valid,
        reset_first=True, row0=0)
    states_s = (state_gdn[l], state_gdn_conv[l], state_lru[l][:, None, :], state_lru_conv[l])
    y_mix_s, gdn_s, gconv_s, lru_s, lconv_s = _mixer(
        proj, states_s, mixer_wts, batch=dec_batch, seq=dec_seq, tt=dec_seq, chunk=dec_seq,
        t_valid=dec_seq, reset_first=False, row0=n_prompt)

    x1, x1p, eidx, gates, rank, counts = _post_mixer(
        y_mix_p, y_mix_s, x_all, w_out[l].astype(BF16), ln_mix_g[l][None], ln_mix_b[l][None],
        router_w[l].T.astype(BF16), router_bias[l][:, None])

    n_pairs = rows * TOP_K
    n_blk = (n_pairs + N_EXPERTS * (MOE_BLOCK - 1) + MOE_BLOCK - 1) // MOE_BLOCK
    n_blk_pad = -(-n_blk // LANES) * LANES
    dest, blk_exp, n_blk_used = _plan(counts, eidx, rank, n_blk_pad)
    xs = _dispatch(dest, x1p, n_blk * MOE_BLOCK)
    ys = _experts(blk_exp[0], n_blk_used[0], xs, expert_w_gate[l], expert_w_up[l], expert_w_down[l])
    y_all = _combine(dest, gates.T, x1, shared_w_gate[l].astype(BF16), shared_w_up[l].astype(BF16),
                     shared_w_down[l].astype(BF16), ln_ffn_g[l][None], ln_ffn_b[l][None], ys)

    y_prompt = y_all[:n_prompt].reshape(batch, seq_pad, D_MODEL)[:, N_META:t_valid]
    y_sample = y_all[n_prompt:].reshape(dec_batch, dec_seq, D_MODEL)
    return (y_prompt, y_sample, gdn_p[None], gconv_p[None], lru_p.reshape(1, batch, LRU_WIDTH), lconv_p[None],
            gdn_s[None], gconv_s[None], lru_s.reshape(1, dec_batch, LRU_WIDTH), lconv_s[None])
```

```python
import functools

import jax
import jax.numpy as jnp
from jax import lax
from jax.experimental import pallas as pl
from jax.experimental.pallas import tpu as pltpu

F32 = jnp.float32
BF16 = jnp.bfloat16
I32 = jnp.int32

LANES = 128
SUBLANES = 8
VMEM_LIMIT = 56 * 1024 * 1024

D_MODEL = 1024
N_META = 16
GDN_WIDTH = 512
GDN_HEADS = 4
HEAD_DIM = 128
LRU_WIDTH = 512
CONV_W = 4
LRU_C = 8.0
N_EXPERTS = 256
TOP_K = 8
N_GROUPS = 8
TOPK_GROUPS = 4
GROUP_SIZE = N_EXPERTS // N_GROUPS
EXPERT_DIM = 256
ROUTED_SCALE = 2.5
MOE_BLOCK = 128
ITEM_BLOCKS = 2
DEEPNORM_ALPHA = 2.0 ** 0.25
LN_EPS = 1e-5
RMS_EPS = 1e-6

QKV = 3 * GDN_WIDTH
OFF_Z = QKV
OFF_LRU = OFF_Z + GDN_WIDTH
OFF_GATE = OFF_LRU + LRU_WIDTH
OFF_BA = OFF_GATE + LRU_WIDTH
PROJ_COLS = OFF_BA + LANES

GDN_CHUNK = 64
PROMPT_TILE = 4 * GDN_CHUNK
META_ROWS = 256
SAMPLE_SEQS_PER_STEP = 8
HIST = CONV_W - 1
HIST_ROW = SUBLANES - HIST


def _cparams(n_axes):
    return pltpu.CompilerParams(dimension_semantics=("arbitrary",) * n_axes, vmem_limit_bytes=VMEM_LIMIT)


def _full_spec(shape):
    nd = len(shape)
    return pl.BlockSpec(shape, lambda *_: (0,) * nd)


def _split3(x):
    hi = x.astype(BF16)
    r = x - hi.astype(F32)
    mid = r.astype(BF16)
    lo = (r - mid.astype(F32)).astype(BF16)
    return hi, mid, lo


def _mm_exact_lhs(m01, x):
    mb = m01.astype(BF16)
    hi, mid, lo = _split3(x)
    dot = functools.partial(jnp.dot, preferred_element_type=F32)
    return dot(mb, hi) + dot(mb, mid) + dot(mb, lo)


def _mm_3pass(a, b):
    a_hi = a.astype(BF16)
    a_lo = (a - a_hi.astype(F32)).astype(BF16)
    b_hi = b.astype(BF16)
    b_lo = (b - b_hi.astype(F32)).astype(BF16)
    dot = functools.partial(jnp.dot, preferred_element_type=F32)
    return dot(a_hi, b_hi) + dot(a_lo, b_hi) + dot(a_hi, b_lo)


def _softplus(x):
    return jnp.maximum(x, 0.0) + jnp.log1p(jnp.exp(-jnp.abs(x)))


def _sigmoid(x):
    return 1.0 / (1.0 + jnp.exp(-x))


def _silu(x):
    return x * _sigmoid(x)


def _layer_norm(x, g, b):
    mu = jnp.mean(x, axis=-1, keepdims=True)
    xc = x - mu
    var = jnp.mean(xc * xc, axis=-1, keepdims=True)
    return xc * lax.rsqrt(var + LN_EPS) * g + b


def _proj_kernel(x_ref, w_ref, o_ref):
    o_ref[...] = jnp.dot(x_ref[...].astype(BF16), w_ref[...], preferred_element_type=F32)


def _proj(x, w):
    rows = x.shape[0]
    tm = 256
    return pl.pallas_call(
        _proj_kernel,
        out_shape=jax.ShapeDtypeStruct((rows, PROJ_COLS), F32),
        grid=(rows // tm,),
        in_specs=[pl.BlockSpec((tm, D_MODEL), lambda i: (i, 0)), _full_spec((D_MODEL, PROJ_COLS))],
        out_specs=pl.BlockSpec((tm, PROJ_COLS), lambda i: (i, 0)),
        compiler_params=_cparams(1),
        name="in_proj",
    )(x, w)


def _bmm(a, b):
    return lax.dot_general(a.astype(BF16), b.astype(BF16), (((2,), (1,)), ((0,), (0,))),
                           preferred_element_type=F32)


def _bmm_nt(a, b):
    return lax.dot_general(a.astype(BF16), b.astype(BF16), (((2,), (2,)), ((0,), (0,))),
                           preferred_element_type=F32)


def _bmm_tn(a, b):
    return lax.dot_general(a.astype(BF16), b.astype(BF16), (((1,), (1,)), ((0,), (0,))),
                           preferred_element_type=F32)


def _scan8(a, b):
    row = lax.broadcasted_iota(I32, a.shape, 0) & (SUBLANES - 1)
    for s in (1, 2, 4):
        keep = row >= s
        a_prev = jnp.where(keep, pltpu.roll(a, s, 0), 1.0)
        b_prev = jnp.where(keep, pltpu.roll(b, s, 0), 0.0)
        b = a * b_prev + b
        a = a * a_prev
    return a, b


def _conv_cols(ext_ref, w_ref, c0, width, ts):
    acc = None
    for j in range(CONV_W):
        term = ext_ref[:, HIST_ROW + j:HIST_ROW + j + ts, c0:c0 + width] * w_ref[j:j + 1, c0:c0 + width]
        acc = term if acc is None else acc + term
    return acc.reshape(acc.shape[0] * ts, width)


def _mixer_kernel(proj_ref, s0_ref, cg0_ref, h0_ref, cl0_ref,
                  gcw_ref, lcw_ref, lcb_ref, wa_ref, wx_ref, ba_ref, bx_ref, lam_ref, lnw_ref, gnw_ref,
                  alog_ref, dtb_ref,
                  y_ref, s_out_ref, cg_out_ref, h_out_ref, cl_out_ref,
                  s_scr, extg, extl, h_scr, q_scr, k_scr, v_scr, bcol_scr, gcol_scr, tv_scr, tk_scr, qd_scr, kd_scr,
                  qkm_scr, o_scr, a_scr, b_scr, hs_scr,
                  *, ns, ts, chunk, t_valid, reset_first):
    t = pl.program_id(1)
    rows = ns * ts
    ncs = ts // chunk
    nb = GDN_HEADS * ns
    t_last = (t_valid - 1) // ts
    off_last = t_valid - t_last * ts

    def put_units(scr, hd, x):
        if ns == 1:
            for cs in range(ncs):
                scr[cs * GDN_HEADS + hd] = x[cs * chunk:(cs + 1) * chunk]
        else:
            scr[hd * ns:(hd + 1) * ns] = x.reshape(ns, chunk, x.shape[-1])

    @pl.when(t == 0)
    def _init():
        for hd in range(GDN_HEADS):
            s_scr[hd * ns:(hd + 1) * ns] = s0_ref[:, hd * HEAD_DIM:(hd + 1) * HEAD_DIM, :]
        extg[:, HIST_ROW:SUBLANES, :] = cg0_ref[...]
        extl[:, HIST_ROW:SUBLANES, :] = cl0_ref[...]
        h_scr[...] = h0_ref[0]

    extg[:, SUBLANES:SUBLANES + ts, :] = proj_ref[:, 0:QKV].reshape(ns, ts, QKV)
    extl[:, SUBLANES:SUBLANES + ts, :] = proj_ref[:, OFF_LRU:OFF_LRU + LRU_WIDTH].reshape(ns, ts, LRU_WIDTH)

    @pl.when(t == t_last)
    def _conv_state():
        cg_out_ref[...] = extg[:, SUBLANES + off_last - HIST:SUBLANES + off_last, :]
        cl_out_ref[...] = extl[:, SUBLANES + off_last - HIST:SUBLANES + off_last, :]

    row_iota = lax.broadcasted_iota(I32, (rows, LANES), 0)
    pos = row_iota + t * ts if ns == 1 else row_iota & (ts - 1)

    for hd in range(GDN_HEADS):
        for part, scr in enumerate((q_scr, k_scr, v_scr)):
            c0 = part * GDN_WIDTH + hd * HEAD_DIM
            x = _silu(_conv_cols(extg, gcw_ref, c0, HEAD_DIM, ts))
            if part < 2:
                x = x * lax.rsqrt(jnp.sum(x * x, axis=-1, keepdims=True) + RMS_EPS)
            if part == 0:
                x = x * (HEAD_DIM ** -0.5)
            put_units(scr, hd, x)

    ba = proj_ref[:, OFF_BA:OFF_BA + LANES]
    valid = pos < t_valid
    beta = jnp.where(valid, _sigmoid(ba), 0.0)
    g = jnp.where(valid, -jnp.exp(alog_ref[...]) * _softplus(ba + dtb_ref[...]), 0.0)
    if chunk == SUBLANES:
        _, gc = _scan8(jnp.ones_like(g), g)
    else:
        ri = lax.broadcasted_iota(I32, (rows, rows), 0)
        ci = lax.broadcasted_iota(I32, (rows, rows), 1)
        seg = jnp.where((ci <= ri) & ((ri // chunk) == (ci // chunk)), 1.0, 0.0)
        gc = _mm_exact_lhs(seg, g)
    for hd in range(GDN_HEADS):
        put_units(bcol_scr, hd, jnp.broadcast_to(beta[:, hd:hd + 1], (rows, LANES)))
        put_units(gcol_scr, hd, jnp.broadcast_to(gc[:, GDN_HEADS + hd:GDN_HEADS + hd + 1], (rows, LANES)))

    q, k, v, bcol, gcol = q_scr[...], k_scr[...], v_scr[...], bcol_scr[...], gcol_scr[...]
    ri = lax.broadcasted_iota(I32, (1, chunk, chunk), 1)
    ci = lax.broadcasted_iota(I32, (1, chunk, chunk), 2)
    causal = ri >= ci
    strict = ri > ci
    eye = jnp.where(ri == ci, 1.0, 0.0)
    g_sq = gcol[:, :, :chunk]
    decay = jnp.exp(jnp.where(causal, g_sq - jnp.swapaxes(g_sq, 1, 2), -jnp.inf))
    kk = _bmm_nt(k, k)
    qkm_scr[...] = _bmm_nt(q, k) * decay
    neg_a = jnp.where(strict, -(bcol[:, :, :chunk] * kk * decay), 0.0)
    inv = eye + neg_a
    power = neg_a
    for _ in range(chunk.bit_length() - 2):
        power = _bmm(power, power)
        inv = inv + _bmm(inv, power)
    eg = jnp.exp(gcol)
    tv_scr[...] = _bmm(inv, bcol * v)
    tk_scr[...] = _bmm(inv, (bcol * eg) * k)
    qd_scr[...] = q * eg
    kd_scr[...] = k * jnp.exp(gcol[:, chunk - 1:chunk, :] - gcol)

    gnw = gnw_ref[...]
    s = s_scr[...]
    for cs in range(ncs):
        step = slice(cs * nb, (cs + 1) * nb)
        u = tv_scr[step] - _bmm(tk_scr[step], s)
        o_scr[...] = _bmm(qd_scr[step], s) + _bmm(qkm_scr[step], u)
        s = jnp.exp(gcol_scr[step, chunk - 1:chunk, :]) * s + _bmm_tn(kd_scr[step], u)
        r0, nr = cs * chunk, ns * chunk
        for hd in range(GDN_HEADS):
            oh = o_scr[hd * ns:(hd + 1) * ns].reshape(nr, HEAD_DIM)
            z = proj_ref[r0:r0 + nr, OFF_Z + hd * HEAD_DIM:OFF_Z + (hd + 1) * HEAD_DIM]
            on = oh * lax.rsqrt(jnp.mean(oh * oh, axis=-1, keepdims=True) + RMS_EPS) * gnw
            y_ref[r0:r0 + nr, hd * HEAD_DIM:(hd + 1) * HEAD_DIM] = on * _silu(z)
    s_scr[...] = s

    @pl.when(t == t_last)
    def _gdn_state():
        for hd in range(GDN_HEADS):
            s_out_ref[:, hd * HEAD_DIM:(hd + 1) * HEAD_DIM, :] = s_scr[hd * ns:(hd + 1) * ns]

    sp_lam = _softplus(-lam_ref[...])
    for p in range(LRU_WIDTH // LANES):
        cols = slice(p * LANES, (p + 1) * LANES)
        xc = _conv_cols(extl, lcw_ref, p * LANES, LANES, ts) + lcb_ref[:, cols]
        r = _sigmoid(_mm_3pass(xc, wa_ref[p]) + ba_ref[:, cols])
        i = _sigmoid(_mm_3pass(xc, wx_ref[p]) + bx_ref[:, cols])
        log_a = -LRU_C * r * sp_lam[:, cols]
        a = jnp.exp(log_a)
        mult = jnp.sqrt(-jnp.tanh(log_a) * (a * a + 1.0))
        if reset_first:
            mult = jnp.where(pos == 0, 1.0, mult)
        a_scr[:, cols] = a
        b_scr[:, cols] = xc * i * mult

    if ns == 1:
        def group_body(gi, h):
            grp = pl.ds(pl.multiple_of(gi * SUBLANES, SUBLANES), SUBLANES)
            a_cum, b_cum = _scan8(a_scr[grp, :], b_scr[grp, :])
            hs = a_cum * h + b_cum
            hs_scr[grp, :] = hs
            return hs[SUBLANES - 1:SUBLANES, :]

        h_scr[...] = lax.fori_loop(0, ts // SUBLANES, group_body, h_scr[...])

        @pl.when(t == t_last)
        def _lru_state():
            h_out_ref[0] = hs_scr[off_last - 1:off_last, :]
    else:
        a_cum, b_cum = _scan8(a_scr[...], b_scr[...])
        h_in = jnp.broadcast_to(h0_ref[...], (ns, ts, LRU_WIDTH)).reshape(rows, LRU_WIDTH)
        hs = a_cum * h_in + b_cum
        hs_scr[...] = hs
        h_out_ref[...] = hs.reshape(ns, ts, LRU_WIDTH)[:, ts - 1:ts, :]

    gate = proj_ref[:, OFF_GATE:OFF_GATE + LRU_WIDTH]
    hg = hs_scr[...] * jax.nn.gelu(gate, approximate=True)
    y_ref[:, GDN_WIDTH:] = hg * lax.rsqrt(jnp.mean(hg * hg, axis=-1, keepdims=True) + RMS_EPS) * lnw_ref[...]

    new_g = extg[:, ts + HIST_ROW:ts + SUBLANES, :]
    new_l = extl[:, ts + HIST_ROW:ts + SUBLANES, :]
    extg[:, HIST_ROW:SUBLANES, :] = new_g
    extl[:, HIST_ROW:SUBLANES, :] = new_l


def _mixer(proj, states, wts, *, name, n_seq, seq, ns, ts, chunk, t_valid, reset_first, shared_init=False):
    assert (ns == 1 or (ts == chunk == seq == SUBLANES)) and n_seq % ns == 0 and seq % ts == 0
    assert not shared_init or ns == 1
    n_t = seq // ts
    rows = ns * ts
    s0, cg0, h0, cl0 = states
    kern = functools.partial(_mixer_kernel, ns=ns, ts=ts, chunk=chunk, t_valid=t_valid, reset_first=reset_first)
    state_shapes = [(ns, GDN_HEADS * HEAD_DIM, HEAD_DIM), (ns, HIST, QKV), (ns, 1, LRU_WIDTH), (ns, HIST, LRU_WIDTH)]
    state_specs = [pl.BlockSpec(s, lambda b, t: (b, 0, 0)) for s in state_shapes]
    init_specs = [pl.BlockSpec(s, lambda b, t: (0, 0, 0)) for s in state_shapes] if shared_init else state_specs
    in_specs = ([pl.BlockSpec((rows, PROJ_COLS), lambda b, t: (b * n_t + t, 0))] + init_specs
                + [_full_spec(w.shape) for w in wts])
    out_shape = [
        jax.ShapeDtypeStruct((n_seq * seq, D_MODEL), F32),
        jax.ShapeDtypeStruct((n_seq, GDN_HEADS * HEAD_DIM, HEAD_DIM), F32),
        jax.ShapeDtypeStruct((n_seq, HIST, QKV), F32),
        jax.ShapeDtypeStruct((n_seq, 1, LRU_WIDTH), F32),
        jax.ShapeDtypeStruct((n_seq, HIST, LRU_WIDTH), F32),
    ]
    out_specs = [pl.BlockSpec((rows, D_MODEL), lambda b, t: (b * n_t + t, 0))] + state_specs
    n_units = GDN_HEADS * rows // chunk
    units = pltpu.VMEM((n_units, chunk, HEAD_DIM), F32)
    scratch = [
        pltpu.VMEM((GDN_HEADS * ns, HEAD_DIM, HEAD_DIM), F32),
        pltpu.VMEM((ns, ts + SUBLANES, QKV), F32),
        pltpu.VMEM((ns, ts + SUBLANES, LRU_WIDTH), F32),
        pltpu.VMEM((1, LRU_WIDTH), F32),
        units, units, units, units, units, units, units, units, units,
        pltpu.VMEM((n_units, chunk, chunk), F32),
        pltpu.VMEM((GDN_HEADS * ns, chunk, HEAD_DIM), F32),
        pltpu.VMEM((rows, LRU_WIDTH), F32),
        pltpu.VMEM((rows, LRU_WIDTH), F32),
        pltpu.VMEM((rows, LRU_WIDTH), F32),
    ]
    return pl.pallas_call(
        kern,
        out_shape=out_shape,
        grid=(n_seq // ns, n_t),
        in_specs=in_specs,
        out_specs=out_specs,
        scratch_shapes=scratch,
        compiler_params=_cparams(2),
        name=name,
    )(proj, s0, cg0, h0, cl0, *wts)


def _route(scores, sel):
    tm = sel.shape[1]
    neg = -jnp.inf
    in_grp = lax.broadcasted_iota(I32, (GROUP_SIZE, tm), 0).astype(F32)
    grp_rows = []
    for gi in range(N_GROUPS):
        blk = sel[gi * GROUP_SIZE:(gi + 1) * GROUP_SIZE, :]
        m1 = jnp.max(blk, axis=0, keepdims=True)
        first = jnp.min(jnp.where(blk == m1, in_grp, float(GROUP_SIZE)), axis=0, keepdims=True)
        m2 = jnp.max(jnp.where(in_grp == first, neg, blk), axis=0, keepdims=True)
        grp_rows.append(m1 + m2)
    grp = jnp.concatenate(grp_rows, axis=0)
    g_iota = lax.broadcasted_iota(I32, (N_GROUPS, tm), 0).astype(F32)
    gmask = jnp.zeros((N_GROUPS, tm), F32)
    for _ in range(TOPK_GROUPS):
        m = jnp.max(grp, axis=0, keepdims=True)
        idx = jnp.min(jnp.where(grp == m, g_iota, float(N_GROUPS)), axis=0, keepdims=True)
        hit = g_iota == idx
        gmask = jnp.where(hit, 1.0, gmask)
        grp = jnp.where(hit, neg, grp)
    gfull = jnp.concatenate(
        [jnp.broadcast_to(gmask[gi:gi + 1, :], (GROUP_SIZE, tm)) for gi in range(N_GROUPS)], axis=0)
    cur = jnp.where(gfull > 0.0, sel, neg)
    e_iota = lax.broadcasted_iota(I32, (N_EXPERTS, tm), 0).astype(F32)
    chosen = jnp.zeros((N_EXPERTS, tm), F32)
    idxs, gates = [], []
    for _ in range(TOP_K):
        m = jnp.max(cur, axis=0, keepdims=True)
        idx = jnp.min(jnp.where(cur == m, e_iota, float(N_EXPERTS)), axis=0, keepdims=True)
        hit = e_iota == idx
        gates.append(jnp.sum(jnp.where(hit, scores, 0.0), axis=0, keepdims=True))
        idxs.append(idx)
        cur = jnp.where(hit, neg, cur)
        chosen = jnp.where(hit, 1.0, chosen)
    return jnp.concatenate(idxs, axis=0), jnp.concatenate(gates, axis=0), chosen, e_iota


def _post_mixer_kernel(y0_ref, y1_ref, y2_ref, x0_ref, x1_ref_in, x2_ref, wo_ref, g_ref, b_ref, rw_ref, rb_ref,
                       x1_ref, eidx_ref, gate_ref, rank_ref, cnt_ref, cnt_scr, *, seg_tiles):
    i = pl.program_id(0)
    t1 = seg_tiles[0]
    t2 = seg_tiles[0] + seg_tiles[1]

    @pl.when(i == 0)
    def _init():
        cnt_scr[...] = jnp.zeros_like(cnt_scr)

    pick = lambda a, b, c: jnp.where(i < t1, a[...], jnp.where(i < t2, b[...], c[...]))
    y = pick(y0_ref, y1_ref, y2_ref)
    x = pick(x0_ref, x1_ref_in, x2_ref)
    mixed = jnp.dot(y.astype(BF16), wo_ref[...], preferred_element_type=F32)
    x1 = _layer_norm(DEEPNORM_ALPHA * x + mixed, g_ref[...], b_ref[...])
    x1_ref[...] = x1
    x1b = x1.astype(BF16)

    logits = lax.dot_general(rw_ref[...], x1b, (((1,), (1,)), ((), ())), preferred_element_type=F32)
    scores = _sigmoid(logits)
    idxs, gates, chosen, e_iota = _route(scores, scores + rb_ref[...])
    gates = gates / jnp.sum(gates, axis=0, keepdims=True) * ROUTED_SCALE

    tm = scores.shape[1]
    ri = lax.broadcasted_iota(I32, (tm, tm), 0)
    ci = lax.broadcasted_iota(I32, (tm, tm), 1)
    before = jnp.where(ri < ci, 1.0, 0.0).astype(BF16)
    rank_all = jnp.dot(chosen.astype(BF16), before, preferred_element_type=F32) + cnt_scr[...]
    ranks = [jnp.sum(jnp.where(e_iota == idxs[j:j + 1, :], rank_all, 0.0), axis=0, keepdims=True)
             for j in range(TOP_K)]
    eidx_ref[...] = idxs.astype(I32)
    gate_ref[...] = gates
    rank_ref[...] = jnp.concatenate(ranks, axis=0).astype(I32)
    cnt_scr[...] = cnt_scr[...] + jnp.sum(chosen, axis=1, keepdims=True)
    cnt_ref[...] = cnt_scr[...]


def _seg_spec(tm, width, seg_tiles, k):
    first = sum(seg_tiles[:k])
    last = seg_tiles[k] - 1
    return pl.BlockSpec((tm, width), lambda i: (jnp.clip(i - first, 0, last), 0))


def _post_mixer(ys, xs, w_out, ln_g, ln_b, router_wt, router_b):
    tm = 256
    assert all(y.shape == x.shape and x.shape[0] % tm == 0 for y, x in zip(ys, xs))
    seg_tiles = tuple(x.shape[0] // tm for x in xs)
    rows = sum(x.shape[0] for x in xs)
    row_spec = lambda w: pl.BlockSpec((tm, w), lambda i: (i, 0))
    slot_spec = pl.BlockSpec((TOP_K, tm), lambda i: (0, i))
    seg_specs = [_seg_spec(tm, D_MODEL, seg_tiles, k) for k in range(3)]
    return pl.pallas_call(
        functools.partial(_post_mixer_kernel, seg_tiles=seg_tiles),
        out_shape=[
            jax.ShapeDtypeStruct((rows, D_MODEL), F32),
            jax.ShapeDtypeStruct((TOP_K, rows), I32),
            jax.ShapeDtypeStruct((TOP_K, rows), F32),
            jax.ShapeDtypeStruct((TOP_K, rows), I32),
            jax.ShapeDtypeStruct((N_EXPERTS, 1), F32),
        ],
        grid=(rows // tm,),
        in_specs=seg_specs + seg_specs + [_full_spec(w_out.shape), _full_spec(ln_g.shape),
                                          _full_spec(ln_b.shape), _full_spec(router_wt.shape),
                                          _full_spec(router_b.shape)],
        out_specs=[row_spec(D_MODEL), slot_spec, slot_spec, slot_spec, _full_spec((N_EXPERTS, 1))],
        scratch_shapes=[pltpu.VMEM((N_EXPERTS, 1), F32)],
        compiler_params=_cparams(1),
        name="post_mixer_router",
    )(*ys, *xs, w_out, ln_g, ln_b, router_wt, router_b)


def _cumsum_experts(lower, x):
    lo = jnp.broadcast_to((x & 63).astype(F32), (N_EXPERTS, LANES)).astype(BF16)
    hi = jnp.broadcast_to((x >> 6).astype(F32), (N_EXPERTS, LANES)).astype(BF16)
    return jnp.dot(lower, hi, preferred_element_type=F32) * 64.0 + jnp.dot(lower, lo, preferred_element_type=F32)


def _plan_kernel(cnt_ref, eidx_ref, rank_ref, dest_ref, iexp_ref, irow_ref, inb_ref, inz_ref, pads_ref, start_scr,
                 *, n_blk, n_item_pad):
    i = pl.program_id(0)

    @pl.when(i == 0)
    def _offsets():
        cnt = cnt_ref[...].astype(I32)
        nblk = (cnt + (MOE_BLOCK - 1)) >> (MOE_BLOCK.bit_length() - 1)
        nitem = (nblk + (ITEM_BLOCKS - 1)) >> (ITEM_BLOCKS.bit_length() - 1)
        ri = lax.broadcasted_iota(I32, (N_EXPERTS, N_EXPERTS), 0)
        ci = lax.broadcasted_iota(I32, (N_EXPERTS, N_EXPERTS), 1)
        lower = jnp.where(ci <= ri, 1.0, 0.0).astype(BF16)
        nblk_f = jnp.broadcast_to(nblk.astype(F32), (N_EXPERTS, LANES))
        nitem_f = jnp.broadcast_to(nitem.astype(F32), (N_EXPERTS, LANES))
        end_blk = _cumsum_experts(lower, nblk)
        end_item = _cumsum_experts(lower, nitem)
        start_blk = end_blk - nblk_f
        start_scr[...] = start_blk * float(MOE_BLOCK)
        item = lax.broadcasted_iota(I32, (1, n_item_pad), 1).astype(F32)
        done = jnp.where(end_item[:, 0:1] <= item, 1.0, 0.0)
        e_of = jnp.minimum(jnp.sum(done, axis=0, keepdims=True), float(N_EXPERTS - 1))
        hit = lax.broadcasted_iota(I32, (N_EXPERTS, n_item_pad), 0).astype(F32) == e_of
        pick = lambda col: jnp.sum(jnp.where(hit, col[:, 0:1], 0.0), axis=0, keepdims=True)
        k = item - pick(end_item - nitem_f)
        nb = jnp.clip(pick(nblk_f) - k * ITEM_BLOCKS, 0.0, float(ITEM_BLOCKS))
        n_used = end_blk[N_EXPERTS - 1:N_EXPERTS, 0:1]
        n_items = end_item[N_EXPERTS - 1:N_EXPERTS, 0:1]
        tail_blk = n_used + (item - n_items) * ITEM_BLOCKS
        nz = jnp.where(item >= n_items, jnp.clip(float(n_blk) - tail_blk, 0.0, float(ITEM_BLOCKS)), 0.0)
        blk0 = jnp.where(item >= n_items, tail_blk, pick(start_blk) + k * ITEM_BLOCKS)
        iexp_ref[...] = e_of.astype(I32)
        irow_ref[...] = (jnp.minimum(blk0, float(n_blk - 1)) * MOE_BLOCK).astype(I32)
        inb_ref[...] = nb.astype(I32)
        inz_ref[...] = nz.astype(I32)
        cnt_f = jnp.broadcast_to(cnt.astype(F32), (N_EXPERTS, LANES))
        pad_start = jnp.transpose(start_blk * float(MOE_BLOCK) + cnt_f)[0:1]
        pad_len = jnp.transpose(nblk_f * float(MOE_BLOCK) - cnt_f)[0:1]
        used = jnp.broadcast_to(n_used, (1, N_EXPERTS))
        pads_ref[...] = jnp.concatenate(
            [pad_start, pad_len, used, jnp.zeros((SUBLANES - 3, N_EXPERTS), F32)], axis=0).astype(I32)

    start = start_scr[:, 0:1]
    eidx = eidx_ref[...]
    tm = eidx.shape[1]
    e_iota = lax.broadcasted_iota(I32, (N_EXPERTS, tm), 0)
    rows = [jnp.sum(jnp.where(e_iota == eidx[j:j + 1, :], start, 0.0), axis=0, keepdims=True)
            for j in range(TOP_K)]
    dest_ref[...] = jnp.concatenate(rows, axis=0).astype(I32) + rank_ref[...]


def _plan(counts, eidx, rank, n_blk, n_item_pad):
    rows = eidx.shape[1]
    tm = MOE_BLOCK
    slot_spec = pl.BlockSpec((TOP_K, tm), lambda i: (0, i))
    item_shape = jax.ShapeDtypeStruct((1, n_item_pad), I32)
    return pl.pallas_call(
        functools.partial(_plan_kernel, n_blk=n_blk, n_item_pad=n_item_pad),
        out_shape=[jax.ShapeDtypeStruct((rows // tm, TOP_K, tm), I32), item_shape, item_shape, item_shape,
                   item_shape, jax.ShapeDtypeStruct((SUBLANES, N_EXPERTS), I32)],
        grid=(rows // tm,),
        in_specs=[_full_spec((N_EXPERTS, 1)), slot_spec, slot_spec],
        out_specs=([pl.BlockSpec((None, TOP_K, tm), lambda i: (i, 0, 0))] + [_full_spec((1, n_item_pad))] * 4
                   + [_full_spec((SUBLANES, N_EXPERTS))]),
        scratch_shapes=[pltpu.VMEM((N_EXPERTS, LANES), F32)],
        compiler_params=_cparams(1),
        name="dispatch_plan",
    )(counts, eidx, rank)


PAD_PIECES = (64, 32, 16, 8)


def _dispatch_kernel(pads_ref, dest_ref, x_ref, xs_ref, dest_smem, zbuf, sem_idx, sem_rows, sem_zero, *, n_blk_total):
    i = pl.program_id(0)
    tm = x_ref.shape[0]
    idx_copy = pltpu.make_async_copy(dest_ref.at[0], dest_smem, sem_idx)
    idx_copy.start()
    idx_copy.wait()

    def row_copy(tok, slot):
        return pltpu.make_async_copy(x_ref.at[pl.ds(tok, 1), :], xs_ref.at[pl.ds(dest_smem[slot, tok], 1), :],
                                     sem_rows)

    def start_body(tok, carry):
        for slot in range(TOP_K):
            row_copy(tok, slot).start(priority=slot % 2)
        return carry

    def wait_body(tok, carry):
        for slot in range(TOP_K):
            row_copy(tok, slot).wait()
        return carry

    def zero_copy(row0, n):
        return pltpu.make_async_copy(zbuf.at[pl.ds(0, n), :], xs_ref.at[pl.ds(row0, n), :], sem_zero)

    def for_each_zero_copy(fn):
        def pad_body(e, carry):
            row0 = pads_ref[e]
            n = pads_ref[N_EXPERTS + e]
            head = (-row0) & (SUBLANES - 1)
            for r in range(SUBLANES - 1):
                @pl.when(r < head)
                def _():
                    fn(zero_copy(row0 + r, 1))
            body = n - head
            for piece in PAD_PIECES:
                @pl.when((body & piece) != 0)
                def _():
                    fn(zero_copy(pl.multiple_of(row0 + head + (body & ~(2 * piece - 1)), SUBLANES), piece))
            return carry

        def tail_body(b, carry):
            fn(zero_copy(pl.multiple_of(b * MOE_BLOCK, MOE_BLOCK), MOE_BLOCK))
            return carry

        lax.fori_loop(0, N_EXPERTS, pad_body, 0)
        lax.fori_loop(pads_ref[2 * N_EXPERTS], n_blk_total, tail_body, 0)

    lax.fori_loop(0, tm, start_body, 0)

    @pl.when(i == pl.num_programs(0) - 1)
    def _zero_fill():
        zbuf[...] = jnp.zeros_like(zbuf)
        for_each_zero_copy(lambda c: c.start())
        for_each_zero_copy(lambda c: c.wait())

    lax.fori_loop(0, tm, wait_body, 0)


def _dispatch(pads, dest, x1, n_blk_total):
    rows, width = x1.shape
    tm = MOE_BLOCK
    grid_spec = pltpu.PrefetchScalarGridSpec(
        num_scalar_prefetch=1,
        grid=(rows // tm,),
        in_specs=[pl.BlockSpec((1, TOP_K, tm), lambda i, p: (i, 0, 0)),
                  pl.BlockSpec((tm, width), lambda i, p: (i, 0))],
        out_specs=pl.BlockSpec(memory_space=pl.ANY),
        scratch_shapes=[pltpu.SMEM((TOP_K, tm), I32), pltpu.VMEM((MOE_BLOCK, width), F32),
                        pltpu.SemaphoreType.DMA, pltpu.SemaphoreType.DMA, pltpu.SemaphoreType.DMA],
    )
    return pl.pallas_call(
        functools.partial(_dispatch_kernel, n_blk_total=n_blk_total),
        out_shape=jax.ShapeDtypeStruct((n_blk_total * MOE_BLOCK, width), F32),
        grid_spec=grid_spec,
        compiler_params=_cparams(1),
        name="moe_dispatch",
    )(pads, dest, x1)


def _experts_kernel(iexp_ref, irow_ref, inb_ref, inz_ref, xs_ref, wg_ref, wu_ref, wd_ref, ys_ref,
                    wg_b, wu_b, wd_b, ybuf, sems):
    w = pl.program_id(0)
    n_items = pl.num_programs(0)
    slot = w & 1

    def out_copy(step, piece):
        row0 = pl.multiple_of(irow_ref[step] + piece * MOE_BLOCK, MOE_BLOCK)
        return pltpu.make_async_copy(ybuf.at[step & 1, pl.ds(piece * MOE_BLOCK, MOE_BLOCK), :],
                                     ys_ref.at[pl.ds(row0, MOE_BLOCK), :], sems.at[step & 1, piece])

    def n_pieces(step):
        return inb_ref[step] + inz_ref[step]

    def wait_step(step):
        for piece in range(ITEM_BLOCKS):
            @pl.when(n_pieces(step) > piece)
            def _():
                out_copy(step, piece).wait()

    @pl.when(w >= 2)
    def _():
        wait_step(jnp.maximum(w - 2, 0))

    @pl.when((w == 0) | (iexp_ref[w] != iexp_ref[jnp.maximum(w - 1, 0)]))
    def _load_weights():
        wg_b[...] = wg_ref[...].astype(BF16)
        wu_b[...] = wu_ref[...].astype(BF16)
        wd_b[...] = wd_ref[...].astype(BF16)

    @pl.when(inb_ref[w] > 0)
    def _ffn():
        x = xs_ref[...].astype(BF16)
        dot = functools.partial(jnp.dot, preferred_element_type=F32)
        hidden = (_silu(dot(x, wg_b[...])) * dot(x, wu_b[...])).astype(BF16)
        ybuf[slot] = dot(hidden, wd_b[...])

    @pl.when(inz_ref[w] > 0)
    def _zero_tail():
        ybuf[slot] = jnp.zeros(ybuf.shape[1:], F32)

    for piece in range(ITEM_BLOCKS):
        @pl.when(n_pieces(w) > piece)
        def _():
            out_copy(w, piece).start()

    @pl.when(w == n_items - 1)
    def _drain():
        @pl.when(w >= 1)
        def _():
            wait_step(jnp.maximum(w - 1, 0))
        wait_step(w)


def _experts(item_exp, item_row, item_nb, item_nz, n_items, xs, w_gate, w_up, w_down, n_rows_out):
    item_rows = ITEM_BLOCKS * MOE_BLOCK
    grid_spec = pltpu.PrefetchScalarGridSpec(
        num_scalar_prefetch=4,
        grid=(n_items,),
        in_specs=[
            pl.BlockSpec((pl.Element(item_rows), pl.Element(D_MODEL)),
                         lambda w, ie, ir, nb, nz: (pl.multiple_of(ir[w], MOE_BLOCK), 0)),
            pl.BlockSpec((None, D_MODEL, EXPERT_DIM), lambda w, ie, ir, nb, nz: (ie[w], 0, 0)),
            pl.BlockSpec((None, D_MODEL, EXPERT_DIM), lambda w, ie, ir, nb, nz: (ie[w], 0, 0)),
            pl.BlockSpec((None, EXPERT_DIM, D_MODEL), lambda w, ie, ir, nb, nz: (ie[w], 0, 0)),
        ],
        out_specs=pl.BlockSpec(memory_space=pl.ANY),
        scratch_shapes=[pltpu.VMEM((D_MODEL, EXPERT_DIM), BF16), pltpu.VMEM((D_MODEL, EXPERT_DIM), BF16),
                        pltpu.VMEM((EXPERT_DIM, D_MODEL), BF16), pltpu.VMEM((2, item_rows, D_MODEL), F32),
                        pltpu.SemaphoreType.DMA((2, ITEM_BLOCKS))],
    )
    return pl.pallas_call(
        _experts_kernel,
        out_shape=jax.ShapeDtypeStruct((n_rows_out, D_MODEL), F32),
        grid_spec=grid_spec,
        compiler_params=_cparams(1),
        name="moe_experts",
    )(item_exp, item_row, item_nb, item_nz, xs, w_gate, w_up, w_down)


def _combine_kernel(dest_ref, gate_ref, x1_ref, sg_ref, su_ref, sd_ref, g_ref, b_ref, ys_ref,
                    out0_ref, out1_ref, dest_smem, ybuf, sem_idx, sem_rows, *, seg_tiles):
    i = pl.program_id(0)
    tm = x1_ref.shape[0]
    idx_copy = pltpu.make_async_copy(dest_ref.at[0], dest_smem, sem_idx)
    idx_copy.start()
    idx_copy.wait()

    def row_copy(tok, slot):
        return pltpu.make_async_copy(ys_ref.at[pl.ds(dest_smem[slot, tok], 1), :],
                                     ybuf.at[slot, pl.ds(tok, 1), :], sem_rows)

    def start_body(tok, carry):
        for slot in range(TOP_K):
            row_copy(tok, slot).start(priority=slot % 2)
        return carry

    def wait_body(tok, carry):
        for slot in range(TOP_K):
            row_copy(tok, slot).wait()
        return carry

    lax.fori_loop(0, tm, start_body, 0)

    x1 = x1_ref[...]
    x1b = x1.astype(BF16)
    dot = functools.partial(jnp.dot, preferred_element_type=F32)
    hidden = (_silu(dot(x1b, sg_ref[...])) * dot(x1b, su_ref[...])).astype(BF16)
    acc = dot(hidden, sd_ref[...])

    lax.fori_loop(0, tm, wait_body, 0)
    gates = gate_ref[...]
    routed = None
    for slot in range(TOP_K):
        term = gates[:, slot:slot + 1] * ybuf[slot]
        routed = term if routed is None else routed + term
    out = _layer_norm(DEEPNORM_ALPHA * x1 + (routed + acc), g_ref[...], b_ref[...])

    @pl.when(i < seg_tiles[0])
    def _():
        out0_ref[...] = out

    @pl.when((i >= seg_tiles[0]) & (i < seg_tiles[0] + seg_tiles[1]))
    def _():
        out1_ref[...] = out


def _combine(dest, gates, x1, sh_gate, sh_up, sh_down, ln_g, ln_b, ys, seg_rows):
    rows = x1.shape[0]
    tm = MOE_BLOCK
    seg_tiles = tuple(r // tm for r in seg_rows)
    assert sum(seg_rows) == rows and all(r % tm == 0 for r in seg_rows)
    return pl.pallas_call(
        functools.partial(_combine_kernel, seg_tiles=seg_tiles),
        out_shape=[jax.ShapeDtypeStruct((seg_rows[0], D_MODEL), F32),
                   jax.ShapeDtypeStruct((seg_rows[1], D_MODEL), F32)],
        grid=(rows // tm,),
        in_specs=[pl.BlockSpec((1, TOP_K, tm), lambda i: (i, 0, 0)),
                  pl.BlockSpec((tm, TOP_K), lambda i: (i, 0)),
                  pl.BlockSpec((tm, D_MODEL), lambda i: (i, 0)),
                  _full_spec(sh_gate.shape), _full_spec(sh_up.shape), _full_spec(sh_down.shape),
                  _full_spec(ln_g.shape), _full_spec(ln_b.shape),
                  pl.BlockSpec(memory_space=pl.ANY)],
        out_specs=[_seg_spec(tm, D_MODEL, seg_tiles, 0), _seg_spec(tm, D_MODEL, seg_tiles, 1)],
        scratch_shapes=[pltpu.SMEM((TOP_K, tm), I32), pltpu.VMEM((TOP_K, tm, D_MODEL), F32),
                        pltpu.SemaphoreType.DMA, pltpu.SemaphoreType.DMA],
        compiler_params=_cparams(1),
        name="moe_combine",
    )(dest, gates, x1, sh_gate, sh_up, sh_down, ln_g, ln_b, ys)


def _pair_blockdiag(w):
    nb, bd, _ = w.shape
    z = jnp.zeros((bd, bd), w.dtype)
    pairs = [jnp.block([[w[2 * p], z], [z, w[2 * p + 1]]]) for p in range(nb // 2)]
    return jnp.stack(pairs, axis=0)


def _lane_row(vals, offset):
    return jnp.zeros((1, LANES), F32).at[0, offset:offset + vals.shape[0]].set(vals.astype(F32))


def kernel(x_prompt, x_sample, state_gdn, state_gdn_conv, state_lru, state_lru_conv, meta_tokens, w_in, gdn_conv_w, gdn_a_log, gdn_dt_bias, gdn_norm_w, lru_conv_w, lru_conv_b, lru_gate_a_w, lru_gate_a_b, lru_gate_x_w, lru_gate_x_b, lru_lambda, lru_norm_w, w_out, ln_mix_g, ln_mix_b, router_w, router_bias, expert_w_gate, expert_w_up, expert_w_down, shared_w_gate, shared_w_up, shared_w_down, ln_ffn_g, ln_ffn_b):
    depth = w_in.shape[0]
    assert depth == 1
    batch, seq, _ = x_prompt.shape
    dec_batch, dec_seq, _ = x_sample.shape
    n_prompt = batch * seq
    n_sample = dec_batch * dec_seq
    assert seq % PROMPT_TILE == 0 and n_sample % 256 == 0 and dec_seq == SUBLANES and N_META >= HIST

    x_segs = (x_prompt.reshape(n_prompt, D_MODEL), x_sample.reshape(n_sample, D_MODEL),
              jnp.concatenate([meta_tokens.astype(F32), jnp.zeros((META_ROWS - N_META, D_MODEL), F32)], axis=0))
    seg_rows = tuple(x.shape[0] for x in x_segs)
    rows = sum(seg_rows)
    l = 0
    w = w_in[l]
    o_ba = 4 * GDN_WIDTH
    o_lru = o_ba + 2 * GDN_HEADS
    w_cat = jnp.concatenate([w[:, :o_ba], w[:, o_lru:], w[:, o_ba:o_lru],
                             jnp.zeros((D_MODEL, LANES - 2 * GDN_HEADS), w.dtype)], axis=1).astype(BF16)
    mixer_wts = (
        gdn_conv_w[l], lru_conv_w[l], lru_conv_b[l][None], _pair_blockdiag(lru_gate_a_w[l]),
        _pair_blockdiag(lru_gate_x_w[l]), lru_gate_a_b[l][None], lru_gate_x_b[l][None], lru_lambda[l][None],
        lru_norm_w[l][None], gdn_norm_w[l][None], _lane_row(gdn_a_log[l], GDN_HEADS),
        _lane_row(gdn_dt_bias[l], GDN_HEADS),
    )

    proj_p, proj_s, proj_m = (_proj(x, w_cat) for x in x_segs)

    gdn_shape = (GDN_HEADS, HEAD_DIM, HEAD_DIM)
    zeros_m = (jnp.zeros((1, GDN_HEADS * HEAD_DIM, HEAD_DIM), F32), jnp.zeros((1, HIST, QKV), F32),
               jnp.zeros((1, 1, LRU_WIDTH), F32), jnp.zeros((1, HIST, LRU_WIDTH), F32))
    y_mix_m, *states_m = _mixer(
        proj_m, zeros_m, mixer_wts, name="mixer_meta", n_seq=1, seq=META_ROWS, ns=1, ts=GDN_CHUNK, chunk=GDN_CHUNK,
        t_valid=N_META, reset_first=True)
    y_mix_p, gdn_p, gconv_p, lru_p, lconv_p = _mixer(
        proj_p, states_m, mixer_wts, name="mixer_prompt", n_seq=batch, seq=seq, ns=1, ts=PROMPT_TILE,
        chunk=GDN_CHUNK, t_valid=seq, reset_first=False, shared_init=True)
    states_s = (state_gdn[l].reshape(dec_batch, GDN_HEADS * HEAD_DIM, HEAD_DIM), state_gdn_conv[l],
                state_lru[l][:, None, :], state_lru_conv[l])
    y_mix_s, gdn_s, gconv_s, lru_s, lconv_s = _mixer(
        proj_s, states_s, mixer_wts, name="mixer_sample", n_seq=dec_batch, seq=dec_seq, ns=SAMPLE_SEQS_PER_STEP,
        ts=dec_seq, chunk=dec_seq, t_valid=dec_seq, reset_first=False)

    x1, eidx, gates, rank, counts = _post_mixer(
        (y_mix_p, y_mix_s, y_mix_m), x_segs, w_out[l].astype(BF16), ln_mix_g[l][None], ln_mix_b[l][None],
        router_w[l].T.astype(BF16), router_bias[l][:, None])

    n_pairs = rows * TOP_K
    n_blk = (n_pairs + N_EXPERTS * (MOE_BLOCK - 1) + MOE_BLOCK - 1) // MOE_BLOCK
    n_items = -(-n_blk // ITEM_BLOCKS) + N_EXPERTS
    n_item_pad = -(-n_items // LANES) * LANES
    dest, item_exp, item_row, item_nb, item_nz, pads = _plan(counts, eidx, rank, n_blk, n_item_pad)
    xs = _dispatch(pads.reshape(-1), dest, x1, n_blk + ITEM_BLOCKS - 1)
    ys = _experts(item_exp[0], item_row[0], item_nb[0], item_nz[0], n_items, xs, expert_w_gate[l],
                  expert_w_up[l], expert_w_down[l], n_blk * MOE_BLOCK)
    y_p, y_s = _combine(dest, gates.T, x1, shared_w_gate[l].astype(BF16), shared_w_up[l].astype(BF16),
                        shared_w_down[l].astype(BF16), ln_ffn_g[l][None], ln_ffn_b[l][None], ys, seg_rows)

    y_prompt = y_p.reshape(batch, seq, D_MODEL)
    y_sample = y_s.reshape(dec_batch, dec_seq, D_MODEL)
    return (y_prompt, y_sample, gdn_p.reshape(1, batch, *gdn_shape), gconv_p[None],
            lru_p.reshape(1, batch, LRU_WIDTH), lconv_p[None], gdn_s.reshape(1, dec_batch, *gdn_shape),
            gconv_s[None], lru_s.reshape(1, dec_batch, LRU_WIDTH), lconv_s[None])
```

```python
import functools

import jax
import jax.numpy as jnp
from jax import lax
from jax.experimental import pallas as pl
from jax.experimental.pallas import tpu as pltpu

F32 = jnp.float32
BF16 = jnp.bfloat16
I32 = jnp.int32

LANES = 128
SUBLANES = 8
VMEM_LIMIT = 56 * 1024 * 1024

D_MODEL = 1024
N_META = 16
GDN_WIDTH = 512
GDN_HEADS = 4
HEAD_DIM = 128
LRU_WIDTH = 512
CONV_W = 4
LRU_C = 8.0
N_EXPERTS = 256
TOP_K = 8
N_GROUPS = 8
TOPK_GROUPS = 4
GROUP_SIZE = N_EXPERTS // N_GROUPS
EXPERT_DIM = 256
ROUTED_SCALE = 2.5
MOE_BLOCK = 128
ITEM_BLOCKS = 2
ROW_DMA_TILE = 128
N_ITEM_TABLES = 6
DEEPNORM_ALPHA = 2.0 ** 0.25
LN_EPS = 1e-5
RMS_EPS = 1e-6

QKV = 3 * GDN_WIDTH
OFF_Z = QKV
OFF_LRU = OFF_Z + GDN_WIDTH
OFF_GATE = OFF_LRU + LRU_WIDTH
OFF_BA = OFF_GATE + LRU_WIDTH
PROJ_COLS = OFF_BA + LANES

GDN_CHUNK = 64
PROMPT_TILE = 4 * GDN_CHUNK
META_ROWS = 256
SAMPLE_SEQS_PER_STEP = 8
HIST = CONV_W - 1
HIST_ROW = SUBLANES - HIST


def _cparams(n_axes):
    return pltpu.CompilerParams(dimension_semantics=("arbitrary",) * n_axes, vmem_limit_bytes=VMEM_LIMIT)


def _full_spec(shape):
    nd = len(shape)
    return pl.BlockSpec(shape, lambda *_: (0,) * nd)


def _split3(x):
    hi = x.astype(BF16)
    r = x - hi.astype(F32)
    mid = r.astype(BF16)
    lo = (r - mid.astype(F32)).astype(BF16)
    return hi, mid, lo


def _mm_exact_lhs(m01, x):
    mb = m01.astype(BF16)
    hi, mid, lo = _split3(x)
    dot = functools.partial(jnp.dot, preferred_element_type=F32)
    return dot(mb, hi) + dot(mb, mid) + dot(mb, lo)


def _mm_3pass(a, b):
    a_hi = a.astype(BF16)
    a_lo = (a - a_hi.astype(F32)).astype(BF16)
    b_hi = b.astype(BF16)
    b_lo = (b - b_hi.astype(F32)).astype(BF16)
    dot = functools.partial(jnp.dot, preferred_element_type=F32)
    return dot(a_hi, b_hi) + dot(a_lo, b_hi) + dot(a_hi, b_lo)


def _softplus(x):
    return jnp.maximum(x, 0.0) + jnp.log1p(jnp.exp(-jnp.abs(x)))


def _sigmoid(x):
    return 1.0 / (1.0 + jnp.exp(-x))


def _silu(x):
    return x * _sigmoid(x)


def _layer_norm(x, g, b):
    mu = jnp.mean(x, axis=-1, keepdims=True)
    xc = x - mu
    var = jnp.mean(xc * xc, axis=-1, keepdims=True)
    return xc * lax.rsqrt(var + LN_EPS) * g + b


def _proj_kernel(x_ref, w_ref, o_ref):
    o_ref[...] = jnp.dot(x_ref[...].astype(BF16), w_ref[...], preferred_element_type=F32)


def _proj(x, w):
    rows = x.shape[0]
    tm = 256
    return pl.pallas_call(
        _proj_kernel,
        out_shape=jax.ShapeDtypeStruct((rows, PROJ_COLS), F32),
        grid=(rows // tm,),
        in_specs=[pl.BlockSpec((tm, D_MODEL), lambda i: (i, 0)), _full_spec((D_MODEL, PROJ_COLS))],
        out_specs=pl.BlockSpec((tm, PROJ_COLS), lambda i: (i, 0)),
        compiler_params=_cparams(1),
        name="in_proj",
    )(x, w)


def _bmm(a, b):
    return lax.dot_general(a.astype(BF16), b.astype(BF16), (((2,), (1,)), ((0,), (0,))),
                           preferred_element_type=F32)


def _bmm_nt(a, b):
    return lax.dot_general(a.astype(BF16), b.astype(BF16), (((2,), (2,)), ((0,), (0,))),
                           preferred_element_type=F32)


def _bmm_tn(a, b):
    return lax.dot_general(a.astype(BF16), b.astype(BF16), (((1,), (1,)), ((0,), (0,))),
                           preferred_element_type=F32)


def _scan8(a, b):
    row = lax.broadcasted_iota(I32, a.shape, 0) & (SUBLANES - 1)
    for s in (1, 2, 4):
        keep = row >= s
        a_prev = jnp.where(keep, pltpu.roll(a, s, 0), 1.0)
        b_prev = jnp.where(keep, pltpu.roll(b, s, 0), 0.0)
        b = a * b_prev + b
        a = a * a_prev
    return a, b


def _conv_cols(ext_ref, w_ref, c0, width, ts):
    acc = None
    for j in range(CONV_W):
        term = ext_ref[:, HIST_ROW + j:HIST_ROW + j + ts, c0:c0 + width] * w_ref[j:j + 1, c0:c0 + width]
        acc = term if acc is None else acc + term
    return acc.reshape(acc.shape[0] * ts, width)


def _mixer_kernel(proj_ref, s0_ref, cg0_ref, h0_ref, cl0_ref,
                  gcw_ref, lcw_ref, lcb_ref, wa_ref, wx_ref, ba_ref, bx_ref, lam_ref, lnw_ref, gnw_ref,
                  alog_ref, dtb_ref,
                  y_ref, s_out_ref, cg_out_ref, h_out_ref, cl_out_ref,
                  s_scr, extg, extl, h_scr, q_scr, k_scr, v_scr, bcol_scr, gcol_scr, tv_scr, tk_scr, qd_scr, kd_scr,
                  qkm_scr, o_scr, a_scr, b_scr, hs_scr,
                  *, ns, ts, chunk, t_valid, reset_first):
    t = pl.program_id(1)
    rows = ns * ts
    ncs = ts // chunk
    nb = GDN_HEADS * ns
    t_last = (t_valid - 1) // ts
    off_last = t_valid - t_last * ts

    def put_units(scr, hd, x):
        if ns == 1:
            for cs in range(ncs):
                scr[cs * GDN_HEADS + hd] = x[cs * chunk:(cs + 1) * chunk]
        else:
            scr[hd * ns:(hd + 1) * ns] = x.reshape(ns, chunk, x.shape[-1])

    @pl.when(t == 0)
    def _init():
        for hd in range(GDN_HEADS):
            s_scr[hd * ns:(hd + 1) * ns] = s0_ref[:, hd * HEAD_DIM:(hd + 1) * HEAD_DIM, :]
        extg[:, HIST_ROW:SUBLANES, :] = cg0_ref[...]
        extl[:, HIST_ROW:SUBLANES, :] = cl0_ref[...]
        h_scr[...] = h0_ref[0]

    extg[:, SUBLANES:SUBLANES + ts, :] = proj_ref[:, 0:QKV].reshape(ns, ts, QKV)
    extl[:, SUBLANES:SUBLANES + ts, :] = proj_ref[:, OFF_LRU:OFF_LRU + LRU_WIDTH].reshape(ns, ts, LRU_WIDTH)

    @pl.when(t == t_last)
    def _conv_state():
        cg_out_ref[...] = extg[:, SUBLANES + off_last - HIST:SUBLANES + off_last, :]
        cl_out_ref[...] = extl[:, SUBLANES + off_last - HIST:SUBLANES + off_last, :]

    row_iota = lax.broadcasted_iota(I32, (rows, LANES), 0)
    pos = row_iota + t * ts if ns == 1 else row_iota & (ts - 1)

    for hd in range(GDN_HEADS):
        for part, scr in enumerate((q_scr, k_scr, v_scr)):
            c0 = part * GDN_WIDTH + hd * HEAD_DIM
            x = _silu(_conv_cols(extg, gcw_ref, c0, HEAD_DIM, ts))
            if part < 2:
                x = x * lax.rsqrt(jnp.sum(x * x, axis=-1, keepdims=True) + RMS_EPS)
            if part == 0:
                x = x * (HEAD_DIM ** -0.5)
            put_units(scr, hd, x)

    ba = proj_ref[:, OFF_BA:OFF_BA + LANES]
    valid = pos < t_valid
    beta = jnp.where(valid, _sigmoid(ba), 0.0)
    g = jnp.where(valid, -jnp.exp(alog_ref[...]) * _softplus(ba + dtb_ref[...]), 0.0)
    if chunk == SUBLANES:
        _, gc = _scan8(jnp.ones_like(g), g)
    else:
        ri = lax.broadcasted_iota(I32, (rows, rows), 0)
        ci = lax.broadcasted_iota(I32, (rows, rows), 1)
        seg = jnp.where((ci <= ri) & ((ri // chunk) == (ci // chunk)), 1.0, 0.0)
        gc = _mm_exact_lhs(seg, g)
    for hd in range(GDN_HEADS):
        put_units(bcol_scr, hd, jnp.broadcast_to(beta[:, hd:hd + 1], (rows, LANES)))
        put_units(gcol_scr, hd, jnp.broadcast_to(gc[:, GDN_HEADS + hd:GDN_HEADS + hd + 1], (rows, LANES)))

    q, k, v, bcol, gcol = q_scr[...], k_scr[...], v_scr[...], bcol_scr[...], gcol_scr[...]
    ri = lax.broadcasted_iota(I32, (1, chunk, chunk), 1)
    ci = lax.broadcasted_iota(I32, (1, chunk, chunk), 2)
    causal = ri >= ci
    strict = ri > ci
    eye = jnp.where(ri == ci, 1.0, 0.0)
    g_sq = gcol[:, :, :chunk]
    decay = jnp.exp(jnp.where(causal, g_sq - jnp.swapaxes(g_sq, 1, 2), -jnp.inf))
    kk = _bmm_nt(k, k)
    qkm_scr[...] = _bmm_nt(q, k) * decay
    neg_a = jnp.where(strict, -(bcol[:, :, :chunk] * kk * decay), 0.0)
    inv = eye + neg_a
    power = neg_a
    for _ in range(chunk.bit_length() - 2):
        power = _bmm(power, power)
        inv = inv + _bmm(inv, power)
    eg = jnp.exp(gcol)
    tv_scr[...] = _bmm(inv, bcol * v)
    tk_scr[...] = _bmm(inv, (bcol * eg) * k)
    qd_scr[...] = q * eg
    kd_scr[...] = k * jnp.exp(gcol[:, chunk - 1:chunk, :] - gcol)

    gnw = gnw_ref[...]
    s = s_scr[...]
    for cs in range(ncs):
        step = slice(cs * nb, (cs + 1) * nb)
        u = tv_scr[step] - _bmm(tk_scr[step], s)
        o_scr[...] = _bmm(qd_scr[step], s) + _bmm(qkm_scr[step], u)
        s = jnp.exp(gcol_scr[step, chunk - 1:chunk, :]) * s + _bmm_tn(kd_scr[step], u)
        r0, nr = cs * chunk, ns * chunk
        for hd in range(GDN_HEADS):
            oh = o_scr[hd * ns:(hd + 1) * ns].reshape(nr, HEAD_DIM)
            z = proj_ref[r0:r0 + nr, OFF_Z + hd * HEAD_DIM:OFF_Z + (hd + 1) * HEAD_DIM]
            on = oh * lax.rsqrt(jnp.mean(oh * oh, axis=-1, keepdims=True) + RMS_EPS) * gnw
            y_ref[r0:r0 + nr, hd * HEAD_DIM:(hd + 1) * HEAD_DIM] = on * _silu(z)
    s_scr[...] = s

    @pl.when(t == t_last)
    def _gdn_state():
        for hd in range(GDN_HEADS):
            s_out_ref[:, hd * HEAD_DIM:(hd + 1) * HEAD_DIM, :] = s_scr[hd * ns:(hd + 1) * ns]

    sp_lam = _softplus(-lam_ref[...])
    for p in range(LRU_WIDTH // LANES):
        cols = slice(p * LANES, (p + 1) * LANES)
        xc = _conv_cols(extl, lcw_ref, p * LANES, LANES, ts) + lcb_ref[:, cols]
        r = _sigmoid(_mm_3pass(xc, wa_ref[p]) + ba_ref[:, cols])
        i = _sigmoid(_mm_3pass(xc, wx_ref[p]) + bx_ref[:, cols])
        log_a = -LRU_C * r * sp_lam[:, cols]
        a = jnp.exp(log_a)
        mult = jnp.sqrt(-jnp.tanh(log_a) * (a * a + 1.0))
        if reset_first:
            mult = jnp.where(pos == 0, 1.0, mult)
        a_scr[:, cols] = a
        b_scr[:, cols] = xc * i * mult

    if ns == 1:
        def group_body(gi, h):
            grp = pl.ds(pl.multiple_of(gi * SUBLANES, SUBLANES), SUBLANES)
            a_cum, b_cum = _scan8(a_scr[grp, :], b_scr[grp, :])
            hs = a_cum * h + b_cum
            hs_scr[grp, :] = hs
            return hs[SUBLANES - 1:SUBLANES, :]

        h_scr[...] = lax.fori_loop(0, ts // SUBLANES, group_body, h_scr[...])

        @pl.when(t == t_last)
        def _lru_state():
            h_out_ref[0] = hs_scr[off_last - 1:off_last, :]
    else:
        a_cum, b_cum = _scan8(a_scr[...], b_scr[...])
        h_in = jnp.broadcast_to(h0_ref[...], (ns, ts, LRU_WIDTH)).reshape(rows, LRU_WIDTH)
        hs = a_cum * h_in + b_cum
        hs_scr[...] = hs
        h_out_ref[...] = hs.reshape(ns, ts, LRU_WIDTH)[:, ts - 1:ts, :]

    gate = proj_ref[:, OFF_GATE:OFF_GATE + LRU_WIDTH]
    hg = hs_scr[...] * jax.nn.gelu(gate, approximate=True)
    y_ref[:, GDN_WIDTH:] = hg * lax.rsqrt(jnp.mean(hg * hg, axis=-1, keepdims=True) + RMS_EPS) * lnw_ref[...]

    new_g = extg[:, ts + HIST_ROW:ts + SUBLANES, :]
    new_l = extl[:, ts + HIST_ROW:ts + SUBLANES, :]
    extg[:, HIST_ROW:SUBLANES, :] = new_g
    extl[:, HIST_ROW:SUBLANES, :] = new_l


def _mixer(proj, states, wts, *, name, n_seq, seq, ns, ts, chunk, t_valid, reset_first, shared_init=False):
    assert (ns == 1 or (ts == chunk == seq == SUBLANES)) and n_seq % ns == 0 and seq % ts == 0
    assert not shared_init or ns == 1
    n_t = seq // ts
    rows = ns * ts
    s0, cg0, h0, cl0 = states
    kern = functools.partial(_mixer_kernel, ns=ns, ts=ts, chunk=chunk, t_valid=t_valid, reset_first=reset_first)
    state_shapes = [(ns, GDN_HEADS * HEAD_DIM, HEAD_DIM), (ns, HIST, QKV), (ns, 1, LRU_WIDTH), (ns, HIST, LRU_WIDTH)]
    state_specs = [pl.BlockSpec(s, lambda b, t: (b, 0, 0)) for s in state_shapes]
    init_specs = [pl.BlockSpec(s, lambda b, t: (0, 0, 0)) for s in state_shapes] if shared_init else state_specs
    in_specs = ([pl.BlockSpec((rows, PROJ_COLS), lambda b, t: (b * n_t + t, 0))] + init_specs
                + [_full_spec(w.shape) for w in wts])
    out_shape = [
        jax.ShapeDtypeStruct((n_seq * seq, D_MODEL), F32),
        jax.ShapeDtypeStruct((n_seq, GDN_HEADS * HEAD_DIM, HEAD_DIM), F32),
        jax.ShapeDtypeStruct((n_seq, HIST, QKV), F32),
        jax.ShapeDtypeStruct((n_seq, 1, LRU_WIDTH), F32),
        jax.ShapeDtypeStruct((n_seq, HIST, LRU_WIDTH), F32),
    ]
    out_specs = [pl.BlockSpec((rows, D_MODEL), lambda b, t: (b * n_t + t, 0))] + state_specs
    n_units = GDN_HEADS * rows // chunk
    units = pltpu.VMEM((n_units, chunk, HEAD_DIM), F32)
    scratch = [
        pltpu.VMEM((GDN_HEADS * ns, HEAD_DIM, HEAD_DIM), F32),
        pltpu.VMEM((ns, ts + SUBLANES, QKV), F32),
        pltpu.VMEM((ns, ts + SUBLANES, LRU_WIDTH), F32),
        pltpu.VMEM((1, LRU_WIDTH), F32),
        units, units, units, units, units, units, units, units, units,
        pltpu.VMEM((n_units, chunk, chunk), F32),
        pltpu.VMEM((GDN_HEADS * ns, chunk, HEAD_DIM), F32),
        pltpu.VMEM((rows, LRU_WIDTH), F32),
        pltpu.VMEM((rows, LRU_WIDTH), F32),
        pltpu.VMEM((rows, LRU_WIDTH), F32),
    ]
    return pl.pallas_call(
        kern,
        out_shape=out_shape,
        grid=(n_seq // ns, n_t),
        in_specs=in_specs,
        out_specs=out_specs,
        scratch_shapes=scratch,
        compiler_params=_cparams(2),
        name=name,
    )(proj, s0, cg0, h0, cl0, *wts)


def _route(scores, sel):
    tm = sel.shape[1]
    neg = -jnp.inf
    in_grp = lax.broadcasted_iota(I32, (GROUP_SIZE, tm), 0).astype(F32)
    grp_rows = []
    for gi in range(N_GROUPS):
        blk = sel[gi * GROUP_SIZE:(gi + 1) * GROUP_SIZE, :]
        m1 = jnp.max(blk, axis=0, keepdims=True)
        first = jnp.min(jnp.where(blk == m1, in_grp, float(GROUP_SIZE)), axis=0, keepdims=True)
        m2 = jnp.max(jnp.where(in_grp == first, neg, blk), axis=0, keepdims=True)
        grp_rows.append(m1 + m2)
    grp = jnp.concatenate(grp_rows, axis=0)
    g_iota = lax.broadcasted_iota(I32, (N_GROUPS, tm), 0).astype(F32)
    gmask = jnp.zeros((N_GROUPS, tm), F32)
    for _ in range(TOPK_GROUPS):
        m = jnp.max(grp, axis=0, keepdims=True)
        idx = jnp.min(jnp.where(grp == m, g_iota, float(N_GROUPS)), axis=0, keepdims=True)
        hit = g_iota == idx
        gmask = jnp.where(hit, 1.0, gmask)
        grp = jnp.where(hit, neg, grp)
    gfull = jnp.concatenate(
        [jnp.broadcast_to(gmask[gi:gi + 1, :], (GROUP_SIZE, tm)) for gi in range(N_GROUPS)], axis=0)
    cur = jnp.where(gfull > 0.0, sel, neg)
    e_iota = lax.broadcasted_iota(I32, (N_EXPERTS, tm), 0).astype(F32)
    chosen = jnp.zeros((N_EXPERTS, tm), F32)
    idxs, gates = [], []
    for _ in range(TOP_K):
        m = jnp.max(cur, axis=0, keepdims=True)
        idx = jnp.min(jnp.where(cur == m, e_iota, float(N_EXPERTS)), axis=0, keepdims=True)
        hit = e_iota == idx
        gates.append(jnp.sum(jnp.where(hit, scores, 0.0), axis=0, keepdims=True))
        idxs.append(idx)
        cur = jnp.where(hit, neg, cur)
        chosen = jnp.where(hit, 1.0, chosen)
    return jnp.concatenate(idxs, axis=0), jnp.concatenate(gates, axis=0), chosen, e_iota


def _post_mixer_kernel(y0_ref, y1_ref, y2_ref, x0_ref, x1_ref_in, x2_ref, wo_ref, g_ref, b_ref, rw_ref, rb_ref,
                       x1_ref, eidx_ref, gate_ref, rank_ref, cnt_ref, cnt_scr, *, seg_tiles):
    i = pl.program_id(0)
    t1 = seg_tiles[0]
    t2 = seg_tiles[0] + seg_tiles[1]

    @pl.when(i == 0)
    def _init():
        cnt_scr[...] = jnp.zeros_like(cnt_scr)

    pick = lambda a, b, c: jnp.where(i < t1, a[...], jnp.where(i < t2, b[...], c[...]))
    y = pick(y0_ref, y1_ref, y2_ref)
    x = pick(x0_ref, x1_ref_in, x2_ref)
    mixed = jnp.dot(y.astype(BF16), wo_ref[...], preferred_element_type=F32)
    x1 = _layer_norm(DEEPNORM_ALPHA * x + mixed, g_ref[...], b_ref[...])
    x1_ref[...] = x1
    x1b = x1.astype(BF16)

    logits = lax.dot_general(rw_ref[...], x1b, (((1,), (1,)), ((), ())), preferred_element_type=F32)
    scores = _sigmoid(logits)
    idxs, gates, chosen, e_iota = _route(scores, scores + rb_ref[...])
    gates = gates / jnp.sum(gates, axis=0, keepdims=True) * ROUTED_SCALE

    tm = scores.shape[1]
    ri = lax.broadcasted_iota(I32, (tm, tm), 0)
    ci = lax.broadcasted_iota(I32, (tm, tm), 1)
    before = jnp.where(ri < ci, 1.0, 0.0).astype(BF16)
    rank_all = jnp.dot(chosen.astype(BF16), before, preferred_element_type=F32) + cnt_scr[...]
    ranks = [jnp.sum(jnp.where(e_iota == idxs[j:j + 1, :], rank_all, 0.0), axis=0, keepdims=True)
             for j in range(TOP_K)]
    eidx_ref[...] = idxs.astype(I32)
    gate_ref[...] = gates
    rank_ref[...] = jnp.concatenate(ranks, axis=0).astype(I32)
    cnt_scr[...] = cnt_scr[...] + jnp.sum(chosen, axis=1, keepdims=True)
    cnt_ref[...] = cnt_scr[...]


def _seg_spec(tm, width, seg_tiles, k):
    first = sum(seg_tiles[:k])
    last = seg_tiles[k] - 1
    return pl.BlockSpec((tm, width), lambda i: (jnp.clip(i - first, 0, last), 0))


def _post_mixer(ys, xs, w_out, ln_g, ln_b, router_wt, router_b):
    tm = 256
    assert all(y.shape == x.shape and x.shape[0] % tm == 0 for y, x in zip(ys, xs))
    seg_tiles = tuple(x.shape[0] // tm for x in xs)
    rows = sum(x.shape[0] for x in xs)
    row_spec = lambda w: pl.BlockSpec((tm, w), lambda i: (i, 0))
    slot_spec = pl.BlockSpec((TOP_K, tm), lambda i: (0, i))
    seg_specs = [_seg_spec(tm, D_MODEL, seg_tiles, k) for k in range(3)]
    return pl.pallas_call(
        functools.partial(_post_mixer_kernel, seg_tiles=seg_tiles),
        out_shape=[
            jax.ShapeDtypeStruct((rows, D_MODEL), F32),
            jax.ShapeDtypeStruct((TOP_K, rows), I32),
            jax.ShapeDtypeStruct((TOP_K, rows), F32),
            jax.ShapeDtypeStruct((TOP_K, rows), I32),
            jax.ShapeDtypeStruct((N_EXPERTS, 1), F32),
        ],
        grid=(rows // tm,),
        in_specs=seg_specs + seg_specs + [_full_spec(w_out.shape), _full_spec(ln_g.shape),
                                          _full_spec(ln_b.shape), _full_spec(router_wt.shape),
                                          _full_spec(router_b.shape)],
        out_specs=[row_spec(D_MODEL), slot_spec, slot_spec, slot_spec, _full_spec((N_EXPERTS, 1))],
        scratch_shapes=[pltpu.VMEM((N_EXPERTS, 1), F32)],
        compiler_params=_cparams(1),
        name="post_mixer_router",
    )(*ys, *xs, w_out, ln_g, ln_b, router_wt, router_b)


def _cumsum_experts(lower, x):
    lo = jnp.broadcast_to((x & 63).astype(F32), (N_EXPERTS, LANES)).astype(BF16)
    hi = jnp.broadcast_to((x >> 6).astype(F32), (N_EXPERTS, LANES)).astype(BF16)
    return jnp.dot(lower, hi, preferred_element_type=F32) * 64.0 + jnp.dot(lower, lo, preferred_element_type=F32)


def _plan_kernel(cnt_ref, eidx_ref, rank_ref, dest_ref, iexp_ref, irow_ref, inb_ref, inz_ref, inext_ref,
                 iwslot_ref, pads_ref, start_scr, *, n_blk, n_item_pad):
    i = pl.program_id(0)

    @pl.when(i == 0)
    def _offsets():
        cnt = cnt_ref[...].astype(I32)
        nblk = (cnt + (MOE_BLOCK - 1)) >> (MOE_BLOCK.bit_length() - 1)
        nitem = (nblk + (ITEM_BLOCKS - 1)) >> (ITEM_BLOCKS.bit_length() - 1)
        ri = lax.broadcasted_iota(I32, (N_EXPERTS, N_EXPERTS), 0)
        ci = lax.broadcasted_iota(I32, (N_EXPERTS, N_EXPERTS), 1)
        lower = jnp.where(ci <= ri, 1.0, 0.0).astype(BF16)
        nblk_f = jnp.broadcast_to(nblk.astype(F32), (N_EXPERTS, LANES))
        nitem_f = jnp.broadcast_to(nitem.astype(F32), (N_EXPERTS, LANES))
        end_blk = _cumsum_experts(lower, nblk)
        end_item = _cumsum_experts(lower, nitem)
        start_blk = end_blk - nblk_f
        start_scr[...] = start_blk * float(MOE_BLOCK)
        item = lax.broadcasted_iota(I32, (1, n_item_pad), 1).astype(F32)
        done = jnp.where(end_item[:, 0:1] <= item, 1.0, 0.0)
        e_of = jnp.minimum(jnp.sum(done, axis=0, keepdims=True), float(N_EXPERTS - 1))
        hit = lax.broadcasted_iota(I32, (N_EXPERTS, n_item_pad), 0).astype(F32) == e_of
        pick = lambda col: jnp.sum(jnp.where(hit, col[:, 0:1], 0.0), axis=0, keepdims=True)
        k = item - pick(end_item - nitem_f)
        nb = jnp.clip(pick(nblk_f) - k * ITEM_BLOCKS, 0.0, float(ITEM_BLOCKS))
        n_used = end_blk[N_EXPERTS - 1:N_EXPERTS, 0:1]
        n_items = end_item[N_EXPERTS - 1:N_EXPERTS, 0:1]
        tail_blk = n_used + (item - n_items) * ITEM_BLOCKS
        nz = jnp.where(item >= n_items, jnp.clip(float(n_blk) - tail_blk, 0.0, float(ITEM_BLOCKS)), 0.0)
        blk0 = jnp.where(item >= n_items, tail_blk, pick(start_blk) + k * ITEM_BLOCKS)
        iexp_ref[...] = e_of.astype(I32)
        irow_ref[...] = (jnp.minimum(blk0, float(n_blk - 1)) * MOE_BLOCK).astype(I32)
        inb_ref[...] = nb.astype(I32)
        inz_ref[...] = nz.astype(I32)
        owner_no = _cumsum_experts(lower, jnp.minimum(nitem, 1))
        no_w = pick(owner_no)
        upto = jnp.sum(jnp.where(owner_no[:, 0:1] <= no_w, 1.0, 0.0), axis=0, keepdims=True)
        inext_ref[...] = jnp.where(upto < float(N_EXPERTS), upto, -1.0).astype(I32)
        iwslot_ref[...] = no_w.astype(I32) & 1
        cnt_f = jnp.broadcast_to(cnt.astype(F32), (N_EXPERTS, LANES))
        pad_start = jnp.transpose(start_blk * float(MOE_BLOCK) + cnt_f)[0:1]
        pad_len = jnp.transpose(nblk_f * float(MOE_BLOCK) - cnt_f)[0:1]
        used = jnp.broadcast_to(n_used, (1, N_EXPERTS))
        pads_ref[...] = jnp.concatenate(
            [pad_start, pad_len, used, jnp.zeros((SUBLANES - 3, N_EXPERTS), F32)], axis=0).astype(I32)

    start = start_scr[:, 0:1]
    eidx = eidx_ref[...]
    tm = eidx.shape[1]
    e_iota = lax.broadcasted_iota(I32, (N_EXPERTS, tm), 0)
    rows = [jnp.sum(jnp.where(e_iota == eidx[j:j + 1, :], start, 0.0), axis=0, keepdims=True)
            for j in range(TOP_K)]
    dest_ref[...] = jnp.concatenate(rows, axis=0).astype(I32) + rank_ref[...]


def _plan(counts, eidx, rank, n_blk, n_item_pad):
    rows = eidx.shape[1]
    tm = ROW_DMA_TILE
    slot_spec = pl.BlockSpec((TOP_K, tm), lambda i: (0, i))
    item_shape = jax.ShapeDtypeStruct((1, n_item_pad), I32)
    return pl.pallas_call(
        functools.partial(_plan_kernel, n_blk=n_blk, n_item_pad=n_item_pad),
        out_shape=([jax.ShapeDtypeStruct((rows // tm, TOP_K, tm), I32)] + [item_shape] * N_ITEM_TABLES
                   + [jax.ShapeDtypeStruct((SUBLANES, N_EXPERTS), I32)]),
        grid=(rows // tm,),
        in_specs=[_full_spec((N_EXPERTS, 1)), slot_spec, slot_spec],
        out_specs=([pl.BlockSpec((None, TOP_K, tm), lambda i: (i, 0, 0))]
                   + [_full_spec((1, n_item_pad))] * N_ITEM_TABLES + [_full_spec((SUBLANES, N_EXPERTS))]),
        scratch_shapes=[pltpu.VMEM((N_EXPERTS, LANES), F32)],
        compiler_params=_cparams(1),
        name="dispatch_plan",
    )(counts, eidx, rank)


PAD_PIECES = (64, 32, 16, 8)


def _dispatch_kernel(pads_ref, dest_ref, x_ref, xs_ref, dest_smem, zbuf, sem_idx, sem_rows, sem_zero, *, n_blk_total):
    i = pl.program_id(0)
    tm = x_ref.shape[0]
    idx_copy = pltpu.make_async_copy(dest_ref.at[0], dest_smem, sem_idx)
    idx_copy.start()
    idx_copy.wait()

    def row_copy(tok, slot):
        return pltpu.make_async_copy(x_ref.at[pl.ds(tok, 1), :], xs_ref.at[pl.ds(dest_smem[slot, tok], 1), :],
                                     sem_rows)

    def start_body(tok, carry):
        for slot in range(TOP_K):
            row_copy(tok, slot).start(priority=slot % 2)
        return carry

    def wait_body(tok, carry):
        for slot in range(TOP_K):
            row_copy(tok, slot).wait()
        return carry

    def zero_copy(row0, n):
        return pltpu.make_async_copy(zbuf.at[pl.ds(0, n), :], xs_ref.at[pl.ds(row0, n), :], sem_zero)

    def for_each_zero_copy(fn):
        def pad_body(e, carry):
            row0 = pads_ref[e]
            n = pads_ref[N_EXPERTS + e]
            head = (-row0) & (SUBLANES - 1)
            for r in range(SUBLANES - 1):
                @pl.when(r < head)
                def _():
                    fn(zero_copy(row0 + r, 1))
            body = n - head
            for piece in PAD_PIECES:
                @pl.when((body & piece) != 0)
                def _():
                    fn(zero_copy(pl.multiple_of(row0 + head + (body & ~(2 * piece - 1)), SUBLANES), piece))
            return carry

        def tail_body(b, carry):
            fn(zero_copy(pl.multiple_of(b * MOE_BLOCK, MOE_BLOCK), MOE_BLOCK))
            return carry

        lax.fori_loop(0, N_EXPERTS, pad_body, 0)
        lax.fori_loop(pads_ref[2 * N_EXPERTS], n_blk_total, tail_body, 0)

    lax.fori_loop(0, tm, start_body, 0)

    @pl.when(i == pl.num_programs(0) - 1)
    def _zero_fill():
        zbuf[...] = jnp.zeros_like(zbuf)
        for_each_zero_copy(lambda c: c.start())
        for_each_zero_copy(lambda c: c.wait())

    lax.fori_loop(0, tm, wait_body, 0)


def _dispatch(pads, dest, x1, n_blk_total):
    rows, width = x1.shape
    tm = ROW_DMA_TILE
    grid_spec = pltpu.PrefetchScalarGridSpec(
        num_scalar_prefetch=1,
        grid=(rows // tm,),
        in_specs=[pl.BlockSpec((1, TOP_K, tm), lambda i, p: (i, 0, 0)),
                  pl.BlockSpec((tm, width), lambda i, p: (i, 0))],
        out_specs=pl.BlockSpec(memory_space=pl.ANY),
        scratch_shapes=[pltpu.SMEM((TOP_K, tm), I32), pltpu.VMEM((MOE_BLOCK, width), F32),
                        pltpu.SemaphoreType.DMA, pltpu.SemaphoreType.DMA, pltpu.SemaphoreType.DMA],
    )
    return pl.pallas_call(
        functools.partial(_dispatch_kernel, n_blk_total=n_blk_total),
        out_shape=jax.ShapeDtypeStruct((n_blk_total * MOE_BLOCK, width), F32),
        grid_spec=grid_spec,
        compiler_params=_cparams(1),
        name="moe_dispatch",
    )(pads, dest, x1)


def _experts_kernel(iexp_ref, irow_ref, inb_ref, inz_ref, inext_ref, iwslot_ref, xs_ref, wg_ref, wu_ref, wd_ref,
                    ys_ref, wg_b, wu_b, wd_b, ybuf, sems, wg_f, wu_f, wd_f, wsems):
    w = pl.program_id(0)
    n_items = pl.num_programs(0)
    slot = w & 1

    def weight_copies(expert, wslot):
        return [pltpu.make_async_copy(src.at[expert], dst.at[wslot], wsems.at[wslot, k])
                for k, (src, dst) in enumerate(((wg_ref, wg_f), (wu_ref, wu_f), (wd_ref, wd_f)))]

    @pl.when(w == 0)
    def _first_weights():
        for c in weight_copies(iexp_ref[0], iwslot_ref[0]):
            c.start()

    def out_copy(step, piece):
        row0 = pl.multiple_of(irow_ref[step] + piece * MOE_BLOCK, MOE_BLOCK)
        return pltpu.make_async_copy(ybuf.at[step & 1, pl.ds(piece * MOE_BLOCK, MOE_BLOCK), :],
                                     ys_ref.at[pl.ds(row0, MOE_BLOCK), :], sems.at[step & 1, piece])

    def n_pieces(step):
        return inb_ref[step] + inz_ref[step]

    def wait_step(step):
        for piece in range(ITEM_BLOCKS):
            @pl.when(n_pieces(step) > piece)
            def _():
                out_copy(step, piece).wait()

    @pl.when(w >= 2)
    def _():
        wait_step(jnp.maximum(w - 2, 0))

    new_expert = (w == 0) | (iexp_ref[w] != iexp_ref[jnp.maximum(w - 1, 0)])

    @pl.when((inb_ref[w] > 0) & new_expert)
    def _load_weights():
        wslot = iwslot_ref[w]
        for c in weight_copies(iexp_ref[w], wslot):
            c.wait()
        wg_b[...] = wg_f[wslot].astype(BF16)
        wu_b[...] = wu_f[wslot].astype(BF16)
        wd_b[...] = wd_f[wslot].astype(BF16)

        @pl.when(inext_ref[w] >= 0)
        def _prefetch():
            for c in weight_copies(inext_ref[w], 1 - wslot):
                c.start()

    @pl.when(inb_ref[w] > 0)
    def _ffn():
        x = xs_ref[...].astype(BF16)
        dot = functools.partial(jnp.dot, preferred_element_type=F32)
        hidden = (_silu(dot(x, wg_b[...])) * dot(x, wu_b[...])).astype(BF16)
        ybuf[slot] = dot(hidden, wd_b[...])

    @pl.when(inz_ref[w] > 0)
    def _zero_tail():
        ybuf[slot] = jnp.zeros(ybuf.shape[1:], F32)

    for piece in range(ITEM_BLOCKS):
        @pl.when(n_pieces(w) > piece)
        def _():
            out_copy(w, piece).start()

    @pl.when(w == n_items - 1)
    def _drain():
        @pl.when(w >= 1)
        def _():
            wait_step(jnp.maximum(w - 1, 0))
        wait_step(w)


def _experts(items, n_items, xs, w_gate, w_up, w_down, n_rows_out):
    item_rows = ITEM_BLOCKS * MOE_BLOCK
    hbm = pl.BlockSpec(memory_space=pl.ANY)
    grid_spec = pltpu.PrefetchScalarGridSpec(
        num_scalar_prefetch=len(items),
        grid=(n_items,),
        in_specs=[
            pl.BlockSpec((pl.Element(item_rows), pl.Element(D_MODEL)),
                         lambda w, ie, ir, *_: (pl.multiple_of(ir[w], MOE_BLOCK), 0)),
            hbm, hbm, hbm,
        ],
        out_specs=hbm,
        scratch_shapes=[pltpu.VMEM((D_MODEL, EXPERT_DIM), BF16), pltpu.VMEM((D_MODEL, EXPERT_DIM), BF16),
                        pltpu.VMEM((EXPERT_DIM, D_MODEL), BF16), pltpu.VMEM((2, item_rows, D_MODEL), F32),
                        pltpu.SemaphoreType.DMA((2, ITEM_BLOCKS)),
                        pltpu.VMEM((2, D_MODEL, EXPERT_DIM), F32), pltpu.VMEM((2, D_MODEL, EXPERT_DIM), F32),
                        pltpu.VMEM((2, EXPERT_DIM, D_MODEL), F32), pltpu.SemaphoreType.DMA((2, 3))],
    )
    return pl.pallas_call(
        _experts_kernel,
        out_shape=jax.ShapeDtypeStruct((n_rows_out, D_MODEL), F32),
        grid_spec=grid_spec,
        compiler_params=_cparams(1),
        name="moe_experts",
    )(*items, xs, w_gate, w_up, w_down)


def _combine_kernel(dest_ref, gate_ref, x1_ref, sg_ref, su_ref, sd_ref, g_ref, b_ref, ys_ref,
                    out0_ref, out1_ref, dest_smem, ybuf, sem_idx, sem_rows, *, seg_tiles):
    i = pl.program_id(0)
    tm = x1_ref.shape[0]
    idx_copy = pltpu.make_async_copy(dest_ref.at[0], dest_smem, sem_idx)
    idx_copy.start()
    idx_copy.wait()

    def row_copy(tok, slot):
        return pltpu.make_async_copy(ys_ref.at[pl.ds(dest_smem[slot, tok], 1), :],
                                     ybuf.at[slot, pl.ds(tok, 1), :], sem_rows)

    def start_body(tok, carry):
        for slot in range(TOP_K):
            row_copy(tok, slot).start(priority=slot % 2)
        return carry

    def wait_body(tok, carry):
        for slot in range(TOP_K):
            row_copy(tok, slot).wait()
        return carry

    lax.fori_loop(0, tm, start_body, 0)

    x1 = x1_ref[...]
    x1b = x1.astype(BF16)
    dot = functools.partial(jnp.dot, preferred_element_type=F32)
    hidden = (_silu(dot(x1b, sg_ref[...])) * dot(x1b, su_ref[...])).astype(BF16)
    acc = dot(hidden, sd_ref[...])

    lax.fori_loop(0, tm, wait_body, 0)
    gates = gate_ref[...]
    routed = None
    for slot in range(TOP_K):
        term = gates[:, slot:slot + 1] * ybuf[slot]
        routed = term if routed is None else routed + term
    out = _layer_norm(DEEPNORM_ALPHA * x1 + (routed + acc), g_ref[...], b_ref[...])

    @pl.when(i < seg_tiles[0])
    def _():
        out0_ref[...] = out

    @pl.when((i >= seg_tiles[0]) & (i < seg_tiles[0] + seg_tiles[1]))
    def _():
        out1_ref[...] = out


def _combine(dest, gates, x1, sh_gate, sh_up, sh_down, ln_g, ln_b, ys, seg_rows):
    rows = x1.shape[0]
    tm = ROW_DMA_TILE
    seg_tiles = tuple(r // tm for r in seg_rows)
    assert sum(seg_rows) == rows and all(r % tm == 0 for r in seg_rows)
    return pl.pallas_call(
        functools.partial(_combine_kernel, seg_tiles=seg_tiles),
        out_shape=[jax.ShapeDtypeStruct((seg_rows[0], D_MODEL), F32),
                   jax.ShapeDtypeStruct((seg_rows[1], D_MODEL), F32)],
        grid=(rows // tm,),
        in_specs=[pl.BlockSpec((1, TOP_K, tm), lambda i: (i, 0, 0)),
                  pl.BlockSpec((tm, TOP_K), lambda i: (i, 0)),
                  pl.BlockSpec((tm, D_MODEL), lambda i: (i, 0)),
                  _full_spec(sh_gate.shape), _full_spec(sh_up.shape), _full_spec(sh_down.shape),
                  _full_spec(ln_g.shape), _full_spec(ln_b.shape),
                  pl.BlockSpec(memory_space=pl.ANY)],
        out_specs=[_seg_spec(tm, D_MODEL, seg_tiles, 0), _seg_spec(tm, D_MODEL, seg_tiles, 1)],
        scratch_shapes=[pltpu.SMEM((TOP_K, tm), I32), pltpu.VMEM((TOP_K, tm, D_MODEL), F32),
                        pltpu.SemaphoreType.DMA, pltpu.SemaphoreType.DMA],
        compiler_params=_cparams(1),
        name="moe_combine",
    )(dest, gates, x1, sh_gate, sh_up, sh_down, ln_g, ln_b, ys)


def _pair_blockdiag(w):
    nb, bd, _ = w.shape
    z = jnp.zeros((bd, bd), w.dtype)
    pairs = [jnp.block([[w[2 * p], z], [z, w[2 * p + 1]]]) for p in range(nb // 2)]
    return jnp.stack(pairs, axis=0)


def _lane_row(vals, offset):
    return jnp.zeros((1, LANES), F32).at[0, offset:offset + vals.shape[0]].set(vals.astype(F32))


def kernel(x_prompt, x_sample, state_gdn, state_gdn_conv, state_lru, state_lru_conv, meta_tokens, w_in, gdn_conv_w, gdn_a_log, gdn_dt_bias, gdn_norm_w, lru_conv_w, lru_conv_b, lru_gate_a_w, lru_gate_a_b, lru_gate_x_w, lru_gate_x_b, lru_lambda, lru_norm_w, w_out, ln_mix_g, ln_mix_b, router_w, router_bias, expert_w_gate, expert_w_up, expert_w_down, shared_w_gate, shared_w_up, shared_w_down, ln_ffn_g, ln_ffn_b):
    depth = w_in.shape[0]
    assert depth == 1
    batch, seq, _ = x_prompt.shape
    dec_batch, dec_seq, _ = x_sample.shape
    n_prompt = batch * seq
    n_sample = dec_batch * dec_seq
    assert seq % PROMPT_TILE == 0 and n_sample % 256 == 0 and dec_seq == SUBLANES and N_META >= HIST

    x_segs = (x_prompt.reshape(n_prompt, D_MODEL), x_sample.reshape(n_sample, D_MODEL),
              jnp.concatenate([meta_tokens.astype(F32), jnp.zeros((META_ROWS - N_META, D_MODEL), F32)], axis=0))
    seg_rows = tuple(x.shape[0] for x in x_segs)
    rows = sum(seg_rows)
    l = 0
    w = w_in[l]
    o_ba = 4 * GDN_WIDTH
    o_lru = o_ba + 2 * GDN_HEADS
    w_cat = jnp.concatenate([w[:, :o_ba], w[:, o_lru:], w[:, o_ba:o_lru],
                             jnp.zeros((D_MODEL, LANES - 2 * GDN_HEADS), w.dtype)], axis=1).astype(BF16)
    mixer_wts = (
        gdn_conv_w[l], lru_conv_w[l], lru_conv_b[l][None], _pair_blockdiag(lru_gate_a_w[l]),
        _pair_blockdiag(lru_gate_x_w[l]), lru_gate_a_b[l][None], lru_gate_x_b[l][None], lru_lambda[l][None],
        lru_norm_w[l][None], gdn_norm_w[l][None], _lane_row(gdn_a_log[l], GDN_HEADS),
        _lane_row(gdn_dt_bias[l], GDN_HEADS),
    )

    proj_p, proj_s, proj_m = (_proj(x, w_cat) for x in x_segs)

    gdn_shape = (GDN_HEADS, HEAD_DIM, HEAD_DIM)
    zeros_m = (jnp.zeros((1, GDN_HEADS * HEAD_DIM, HEAD_DIM), F32), jnp.zeros((1, HIST, QKV), F32),
               jnp.zeros((1, 1, LRU_WIDTH), F32), jnp.zeros((1, HIST, LRU_WIDTH), F32))
    y_mix_m, *states_m = _mixer(
        proj_m, zeros_m, mixer_wts, name="mixer_meta", n_seq=1, seq=META_ROWS, ns=1, ts=GDN_CHUNK, chunk=GDN_CHUNK,
        t_valid=N_META, reset_first=True)
    y_mix_p, gdn_p, gconv_p, lru_p, lconv_p = _mixer(
        proj_p, states_m, mixer_wts, name="mixer_prompt", n_seq=batch, seq=seq, ns=1, ts=PROMPT_TILE,
        chunk=GDN_CHUNK, t_valid=seq, reset_first=False, shared_init=True)
    states_s = (state_gdn[l].reshape(dec_batch, GDN_HEADS * HEAD_DIM, HEAD_DIM), state_gdn_conv[l],
                state_lru[l][:, None, :], state_lru_conv[l])
    y_mix_s, gdn_s, gconv_s, lru_s, lconv_s = _mixer(
        proj_s, states_s, mixer_wts, name="mixer_sample", n_seq=dec_batch, seq=dec_seq, ns=SAMPLE_SEQS_PER_STEP,
        ts=dec_seq, chunk=dec_seq, t_valid=dec_seq, reset_first=False)

    x1, eidx, gates, rank, counts = _post_mixer(
        (y_mix_p, y_mix_s, y_mix_m), x_segs, w_out[l].astype(BF16), ln_mix_g[l][None], ln_mix_b[l][None],
        router_w[l].T.astype(BF16), router_bias[l][:, None])

    n_pairs = rows * TOP_K
    n_blk = (n_pairs + N_EXPERTS * (MOE_BLOCK - 1) + MOE_BLOCK - 1) // MOE_BLOCK
    n_items = -(-n_blk // ITEM_BLOCKS) + N_EXPERTS
    n_item_pad = -(-n_items // LANES) * LANES
    dest, *items, pads = _plan(counts, eidx, rank, n_blk, n_item_pad)
    xs = _dispatch(pads.reshape(-1), dest, x1, n_blk + ITEM_BLOCKS - 1)
    ys = _experts([t[0] for t in items], n_items, xs, expert_w_gate[l], expert_w_up[l], expert_w_down[l],
                  n_blk * MOE_BLOCK)
    y_p, y_s = _combine(dest, gates.T, x1, shared_w_gate[l].astype(BF16), shared_w_up[l].astype(BF16),
                        shared_w_down[l].astype(BF16), ln_ffn_g[l][None], ln_ffn_b[l][None], ys, seg_rows)

    y_prompt = y_p.reshape(batch, seq, D_MODEL)
    y_sample = y_s.reshape(dec_batch, dec_seq, D_MODEL)
    return (y_prompt, y_sample, gdn_p.reshape(1, batch, *gdn_shape), gconv_p[None],
            lru_p.reshape(1, batch, LRU_WIDTH), lconv_p[None], gdn_s.reshape(1, dec_batch, *gdn_shape),
            gconv_s[None], lru_s.reshape(1, dec_batch, LRU_WIDTH), lconv_s[None])
```

```python
import functools

import jax
import jax.numpy as jnp
from jax import lax
from jax.experimental import pallas as pl
from jax.experimental.pallas import tpu as pltpu

F32 = jnp.float32
BF16 = jnp.bfloat16
I32 = jnp.int32

LANES = 128
SUBLANES = 8
VMEM_LIMIT = 56 * 1024 * 1024

D_MODEL = 1024
N_META = 16
GDN_WIDTH = 512
GDN_HEADS = 4
HEAD_DIM = 128
LRU_WIDTH = 512
CONV_W = 4
LRU_C = 8.0
N_EXPERTS = 256
TOP_K = 8
N_GROUPS = 8
TOPK_GROUPS = 4
GROUP_SIZE = N_EXPERTS // N_GROUPS
EXPERT_DIM = 256
ROUTED_SCALE = 2.5
MOE_BLOCK = 128
ITEM_BLOCKS = 4
FFN_ROWS = 256
ROW_DMA_TILE = 128
N_ITEM_TABLES = 6
DEEPNORM_ALPHA = 2.0 ** 0.25
LN_EPS = 1e-5
RMS_EPS = 1e-6

QKV = 3 * GDN_WIDTH
OFF_Z = QKV
OFF_LRU = OFF_Z + GDN_WIDTH
OFF_GATE = OFF_LRU + LRU_WIDTH
OFF_BA = OFF_GATE + LRU_WIDTH
PROJ_COLS = OFF_BA + LANES

GDN_CHUNK = 64
PROMPT_TILE = 4 * GDN_CHUNK
META_ROWS = 256
SAMPLE_SEQS_PER_STEP = 8
HIST = CONV_W - 1
HIST_ROW = SUBLANES - HIST


def _cparams(n_axes):
    return pltpu.CompilerParams(dimension_semantics=("arbitrary",) * n_axes, vmem_limit_bytes=VMEM_LIMIT)


def _full_spec(shape):
    nd = len(shape)
    return pl.BlockSpec(shape, lambda *_: (0,) * nd)


def _split3(x):
    hi = x.astype(BF16)
    r = x - hi.astype(F32)
    mid = r.astype(BF16)
    lo = (r - mid.astype(F32)).astype(BF16)
    return hi, mid, lo


def _mm_exact_lhs(m01, x):
    mb = m01.astype(BF16)
    hi, mid, lo = _split3(x)
    dot = functools.partial(jnp.dot, preferred_element_type=F32)
    return dot(mb, hi) + dot(mb, mid) + dot(mb, lo)


def _mm_3pass(a, b):
    a_hi = a.astype(BF16)
    a_lo = (a - a_hi.astype(F32)).astype(BF16)
    b_hi = b.astype(BF16)
    b_lo = (b - b_hi.astype(F32)).astype(BF16)
    dot = functools.partial(jnp.dot, preferred_element_type=F32)
    return dot(a_hi, b_hi) + dot(a_lo, b_hi) + dot(a_hi, b_lo)


def _softplus(x):
    return jnp.maximum(x, 0.0) + jnp.log1p(jnp.exp(-jnp.abs(x)))


def _sigmoid(x):
    return 1.0 / (1.0 + jnp.exp(-x))


def _silu(x):
    return x * _sigmoid(x)


def _layer_norm(x, g, b):
    mu = jnp.mean(x, axis=-1, keepdims=True)
    xc = x - mu
    var = jnp.mean(xc * xc, axis=-1, keepdims=True)
    return xc * lax.rsqrt(var + LN_EPS) * g + b


def _proj_kernel(x_ref, w_ref, o_ref):
    o_ref[...] = jnp.dot(x_ref[...].astype(BF16), w_ref[...], preferred_element_type=F32)


def _proj(x, w):
    rows = x.shape[0]
    tm = 256
    return pl.pallas_call(
        _proj_kernel,
        out_shape=jax.ShapeDtypeStruct((rows, PROJ_COLS), F32),
        grid=(rows // tm,),
        in_specs=[pl.BlockSpec((tm, D_MODEL), lambda i: (i, 0)), _full_spec((D_MODEL, PROJ_COLS))],
        out_specs=pl.BlockSpec((tm, PROJ_COLS), lambda i: (i, 0)),
        compiler_params=_cparams(1),
        name="in_proj",
    )(x, w)


def _bmm(a, b):
    return lax.dot_general(a.astype(BF16), b.astype(BF16), (((2,), (1,)), ((0,), (0,))),
                           preferred_element_type=F32)


def _bmm_nt(a, b):
    return lax.dot_general(a.astype(BF16), b.astype(BF16), (((2,), (2,)), ((0,), (0,))),
                           preferred_element_type=F32)


def _bmm_tn(a, b):
    return lax.dot_general(a.astype(BF16), b.astype(BF16), (((1,), (1,)), ((0,), (0,))),
                           preferred_element_type=F32)


def _scan8(a, b):
    row = lax.broadcasted_iota(I32, a.shape, 0) & (SUBLANES - 1)
    for s in (1, 2, 4):
        keep = row >= s
        a_prev = jnp.where(keep, pltpu.roll(a, s, 0), 1.0)
        b_prev = jnp.where(keep, pltpu.roll(b, s, 0), 0.0)
        b = a * b_prev + b
        a = a * a_prev
    return a, b


def _conv_cols(ext_ref, w_ref, c0, width, ts):
    acc = None
    for j in range(CONV_W):
        term = ext_ref[:, HIST_ROW + j:HIST_ROW + j + ts, c0:c0 + width] * w_ref[j:j + 1, c0:c0 + width]
        acc = term if acc is None else acc + term
    return acc.reshape(acc.shape[0] * ts, width)


def _mixer_kernel(proj_ref, s0_ref, cg0_ref, h0_ref, cl0_ref,
                  gcw_ref, lcw_ref, lcb_ref, wa_ref, wx_ref, ba_ref, bx_ref, lam_ref, lnw_ref, gnw_ref,
                  alog_ref, dtb_ref,
                  y_ref, s_out_ref, cg_out_ref, h_out_ref, cl_out_ref,
                  s_scr, extg, extl, h_scr, q_scr, k_scr, v_scr, bcol_scr, gcol_scr, tv_scr, tk_scr, qd_scr, kd_scr,
                  qkm_scr, o_scr, a_scr, b_scr, hs_scr,
                  *, ns, ts, chunk, t_valid, reset_first):
    t = pl.program_id(1)
    rows = ns * ts
    ncs = ts // chunk
    nb = GDN_HEADS * ns
    t_last = (t_valid - 1) // ts
    off_last = t_valid - t_last * ts

    def put_units(scr, hd, x):
        if ns == 1:
            for cs in range(ncs):
                scr[cs * GDN_HEADS + hd] = x[cs * chunk:(cs + 1) * chunk]
        else:
            scr[hd * ns:(hd + 1) * ns] = x.reshape(ns, chunk, x.shape[-1])

    @pl.when(t == 0)
    def _init():
        for hd in range(GDN_HEADS):
            s_scr[hd * ns:(hd + 1) * ns] = s0_ref[:, hd * HEAD_DIM:(hd + 1) * HEAD_DIM, :]
        extg[:, HIST_ROW:SUBLANES, :] = cg0_ref[...]
        extl[:, HIST_ROW:SUBLANES, :] = cl0_ref[...]
        h_scr[...] = h0_ref[0]

    extg[:, SUBLANES:SUBLANES + ts, :] = proj_ref[:, 0:QKV].reshape(ns, ts, QKV)
    extl[:, SUBLANES:SUBLANES + ts, :] = proj_ref[:, OFF_LRU:OFF_LRU + LRU_WIDTH].reshape(ns, ts, LRU_WIDTH)

    @pl.when(t == t_last)
    def _conv_state():
        cg_out_ref[...] = extg[:, SUBLANES + off_last - HIST:SUBLANES + off_last, :]
        cl_out_ref[...] = extl[:, SUBLANES + off_last - HIST:SUBLANES + off_last, :]

    row_iota = lax.broadcasted_iota(I32, (rows, LANES), 0)
    pos = row_iota + t * ts if ns == 1 else row_iota & (ts - 1)

    for hd in range(GDN_HEADS):
        for part, scr in enumerate((q_scr, k_scr, v_scr)):
            c0 = part * GDN_WIDTH + hd * HEAD_DIM
            x = _silu(_conv_cols(extg, gcw_ref, c0, HEAD_DIM, ts))
            if part < 2:
                x = x * lax.rsqrt(jnp.sum(x * x, axis=-1, keepdims=True) + RMS_EPS)
            if part == 0:
                x = x * (HEAD_DIM ** -0.5)
            put_units(scr, hd, x)

    ba = proj_ref[:, OFF_BA:OFF_BA + LANES]
    valid = pos < t_valid
    beta = jnp.where(valid, _sigmoid(ba), 0.0)
    g = jnp.where(valid, -jnp.exp(alog_ref[...]) * _softplus(ba + dtb_ref[...]), 0.0)
    if chunk == SUBLANES:
        _, gc = _scan8(jnp.ones_like(g), g)
    else:
        ri = lax.broadcasted_iota(I32, (rows, rows), 0)
        ci = lax.broadcasted_iota(I32, (rows, rows), 1)
        seg = jnp.where((ci <= ri) & ((ri // chunk) == (ci // chunk)), 1.0, 0.0)
        gc = _mm_exact_lhs(seg, g)
    for hd in range(GDN_HEADS):
        put_units(bcol_scr, hd, jnp.broadcast_to(beta[:, hd:hd + 1], (rows, LANES)))
        put_units(gcol_scr, hd, jnp.broadcast_to(gc[:, GDN_HEADS + hd:GDN_HEADS + hd + 1], (rows, LANES)))

    q, k, v, bcol, gcol = q_scr[...], k_scr[...], v_scr[...], bcol_scr[...], gcol_scr[...]
    ri = lax.broadcasted_iota(I32, (1, chunk, chunk), 1)
    ci = lax.broadcasted_iota(I32, (1, chunk, chunk), 2)
    causal = ri >= ci
    strict = ri > ci
    eye = jnp.where(ri == ci, 1.0, 0.0)
    g_sq = gcol[:, :, :chunk]
    decay = jnp.exp(jnp.where(causal, g_sq - jnp.swapaxes(g_sq, 1, 2), -jnp.inf))
    kk = _bmm_nt(k, k)
    qkm_scr[...] = _bmm_nt(q, k) * decay
    neg_a = jnp.where(strict, -(bcol[:, :, :chunk] * kk * decay), 0.0)
    inv = eye + neg_a
    power = neg_a
    for _ in range(chunk.bit_length() - 2):
        power = _bmm(power, power)
        inv = inv + _bmm(inv, power)
    eg = jnp.exp(gcol)
    tv_scr[...] = _bmm(inv, bcol * v)
    tk_scr[...] = _bmm(inv, (bcol * eg) * k)
    qd_scr[...] = q * eg
    kd_scr[...] = k * jnp.exp(gcol[:, chunk - 1:chunk, :] - gcol)

    gnw = gnw_ref[...]
    s = s_scr[...]
    for cs in range(ncs):
        step = slice(cs * nb, (cs + 1) * nb)
        u = tv_scr[step] - _bmm(tk_scr[step], s)
        o_scr[...] = _bmm(qd_scr[step], s) + _bmm(qkm_scr[step], u)
        s = jnp.exp(gcol_scr[step, chunk - 1:chunk, :]) * s + _bmm_tn(kd_scr[step], u)
        r0, nr = cs * chunk, ns * chunk
        for hd in range(GDN_HEADS):
            oh = o_scr[hd * ns:(hd + 1) * ns].reshape(nr, HEAD_DIM)
            z = proj_ref[r0:r0 + nr, OFF_Z + hd * HEAD_DIM:OFF_Z + (hd + 1) * HEAD_DIM]
            on = oh * lax.rsqrt(jnp.mean(oh * oh, axis=-1, keepdims=True) + RMS_EPS) * gnw
            y_ref[r0:r0 + nr, hd * HEAD_DIM:(hd + 1) * HEAD_DIM] = on * _silu(z)
    s_scr[...] = s

    @pl.when(t == t_last)
    def _gdn_state():
        for hd in range(GDN_HEADS):
            s_out_ref[:, hd * HEAD_DIM:(hd + 1) * HEAD_DIM, :] = s_scr[hd * ns:(hd + 1) * ns]

    sp_lam = _softplus(-lam_ref[...])
    for p in range(LRU_WIDTH // LANES):
        cols = slice(p * LANES, (p + 1) * LANES)
        xc = _conv_cols(extl, lcw_ref, p * LANES, LANES, ts) + lcb_ref[:, cols]
        r = _sigmoid(_mm_3pass(xc, wa_ref[p]) + ba_ref[:, cols])
        i = _sigmoid(_mm_3pass(xc, wx_ref[p]) + bx_ref[:, cols])
        log_a = -LRU_C * r * sp_lam[:, cols]
        a = jnp.exp(log_a)
        mult = jnp.sqrt(-jnp.tanh(log_a) * (a * a + 1.0))
        if reset_first:
            mult = jnp.where(pos == 0, 1.0, mult)
        a_scr[:, cols] = a
        b_scr[:, cols] = xc * i * mult

    if ns == 1:
        def group_body(gi, h):
            grp = pl.ds(pl.multiple_of(gi * SUBLANES, SUBLANES), SUBLANES)
            a_cum, b_cum = _scan8(a_scr[grp, :], b_scr[grp, :])
            hs = a_cum * h + b_cum
            hs_scr[grp, :] = hs
            return hs[SUBLANES - 1:SUBLANES, :]

        h_scr[...] = lax.fori_loop(0, ts // SUBLANES, group_body, h_scr[...])

        @pl.when(t == t_last)
        def _lru_state():
            h_out_ref[0] = hs_scr[off_last - 1:off_last, :]
    else:
        a_cum, b_cum = _scan8(a_scr[...], b_scr[...])
        h_in = jnp.broadcast_to(h0_ref[...], (ns, ts, LRU_WIDTH)).reshape(rows, LRU_WIDTH)
        hs = a_cum * h_in + b_cum
        hs_scr[...] = hs
        h_out_ref[...] = hs.reshape(ns, ts, LRU_WIDTH)[:, ts - 1:ts, :]

    gate = proj_ref[:, OFF_GATE:OFF_GATE + LRU_WIDTH]
    hg = hs_scr[...] * jax.nn.gelu(gate, approximate=True)
    y_ref[:, GDN_WIDTH:] = hg * lax.rsqrt(jnp.mean(hg * hg, axis=-1, keepdims=True) + RMS_EPS) * lnw_ref[...]

    new_g = extg[:, ts + HIST_ROW:ts + SUBLANES, :]
    new_l = extl[:, ts + HIST_ROW:ts + SUBLANES, :]
    extg[:, HIST_ROW:SUBLANES, :] = new_g
    extl[:, HIST_ROW:SUBLANES, :] = new_l


def _mixer(proj, states, wts, *, name, n_seq, seq, ns, ts, chunk, t_valid, reset_first, shared_init=False):
    assert (ns == 1 or (ts == chunk == seq == SUBLANES)) and n_seq % ns == 0 and seq % ts == 0
    assert not shared_init or ns == 1
    n_t = seq // ts
    rows = ns * ts
    s0, cg0, h0, cl0 = states
    kern = functools.partial(_mixer_kernel, ns=ns, ts=ts, chunk=chunk, t_valid=t_valid, reset_first=reset_first)
    state_shapes = [(ns, GDN_HEADS * HEAD_DIM, HEAD_DIM), (ns, HIST, QKV), (ns, 1, LRU_WIDTH), (ns, HIST, LRU_WIDTH)]
    state_specs = [pl.BlockSpec(s, lambda b, t: (b, 0, 0)) for s in state_shapes]
    init_specs = [pl.BlockSpec(s, lambda b, t: (0, 0, 0)) for s in state_shapes] if shared_init else state_specs
    in_specs = ([pl.BlockSpec((rows, PROJ_COLS), lambda b, t: (b * n_t + t, 0))] + init_specs
                + [_full_spec(w.shape) for w in wts])
    out_shape = [
        jax.ShapeDtypeStruct((n_seq * seq, D_MODEL), F32),
        jax.ShapeDtypeStruct((n_seq, GDN_HEADS * HEAD_DIM, HEAD_DIM), F32),
        jax.ShapeDtypeStruct((n_seq, HIST, QKV), F32),
        jax.ShapeDtypeStruct((n_seq, 1, LRU_WIDTH), F32),
        jax.ShapeDtypeStruct((n_seq, HIST, LRU_WIDTH), F32),
    ]
    out_specs = [pl.BlockSpec((rows, D_MODEL), lambda b, t: (b * n_t + t, 0))] + state_specs
    n_units = GDN_HEADS * rows // chunk
    units = pltpu.VMEM((n_units, chunk, HEAD_DIM), F32)
    scratch = [
        pltpu.VMEM((GDN_HEADS * ns, HEAD_DIM, HEAD_DIM), F32),
        pltpu.VMEM((ns, ts + SUBLANES, QKV), F32),
        pltpu.VMEM((ns, ts + SUBLANES, LRU_WIDTH), F32),
        pltpu.VMEM((1, LRU_WIDTH), F32),
        units, units, units, units, units, units, units, units, units,
        pltpu.VMEM((n_units, chunk, chunk), F32),
        pltpu.VMEM((GDN_HEADS * ns, chunk, HEAD_DIM), F32),
        pltpu.VMEM((rows, LRU_WIDTH), F32),
        pltpu.VMEM((rows, LRU_WIDTH), F32),
        pltpu.VMEM((rows, LRU_WIDTH), F32),
    ]
    return pl.pallas_call(
        kern,
        out_shape=out_shape,
        grid=(n_seq // ns, n_t),
        in_specs=in_specs,
        out_specs=out_specs,
        scratch_shapes=scratch,
        compiler_params=_cparams(2),
        name=name,
    )(proj, s0, cg0, h0, cl0, *wts)


def _route(scores, sel):
    tm = sel.shape[1]
    neg = -jnp.inf
    in_grp = lax.broadcasted_iota(I32, (GROUP_SIZE, tm), 0).astype(F32)
    grp_rows = []
    for gi in range(N_GROUPS):
        blk = sel[gi * GROUP_SIZE:(gi + 1) * GROUP_SIZE, :]
        m1 = jnp.max(blk, axis=0, keepdims=True)
        first = jnp.min(jnp.where(blk == m1, in_grp, float(GROUP_SIZE)), axis=0, keepdims=True)
        m2 = jnp.max(jnp.where(in_grp == first, neg, blk), axis=0, keepdims=True)
        grp_rows.append(m1 + m2)
    grp = jnp.concatenate(grp_rows, axis=0)
    g_iota = lax.broadcasted_iota(I32, (N_GROUPS, tm), 0).astype(F32)
    gmask = jnp.zeros((N_GROUPS, tm), F32)
    for _ in range(TOPK_GROUPS):
        m = jnp.max(grp, axis=0, keepdims=True)
        idx = jnp.min(jnp.where(grp == m, g_iota, float(N_GROUPS)), axis=0, keepdims=True)
        hit = g_iota == idx
        gmask = jnp.where(hit, 1.0, gmask)
        grp = jnp.where(hit, neg, grp)
    gfull = jnp.concatenate(
        [jnp.broadcast_to(gmask[gi:gi + 1, :], (GROUP_SIZE, tm)) for gi in range(N_GROUPS)], axis=0)
    cur = jnp.where(gfull > 0.0, sel, neg)
    e_iota = lax.broadcasted_iota(I32, (N_EXPERTS, tm), 0).astype(F32)
    chosen = jnp.zeros((N_EXPERTS, tm), F32)
    idxs, gates = [], []
    for _ in range(TOP_K):
        m = jnp.max(cur, axis=0, keepdims=True)
        idx = jnp.min(jnp.where(cur == m, e_iota, float(N_EXPERTS)), axis=0, keepdims=True)
        hit = e_iota == idx
        gates.append(jnp.sum(jnp.where(hit, scores, 0.0), axis=0, keepdims=True))
        idxs.append(idx)
        cur = jnp.where(hit, neg, cur)
        chosen = jnp.where(hit, 1.0, chosen)
    return jnp.concatenate(idxs, axis=0), jnp.concatenate(gates, axis=0), chosen, e_iota


def _post_mixer_kernel(y0_ref, y1_ref, y2_ref, x0_ref, x1_ref_in, x2_ref, wo_ref, g_ref, b_ref, rw_ref, rb_ref,
                       x1_ref, eidx_ref, gate_ref, rank_ref, cnt_ref, cnt_scr, *, seg_tiles):
    i = pl.program_id(0)
    t1 = seg_tiles[0]
    t2 = seg_tiles[0] + seg_tiles[1]

    @pl.when(i == 0)
    def _init():
        cnt_scr[...] = jnp.zeros_like(cnt_scr)

    pick = lambda a, b, c: jnp.where(i < t1, a[...], jnp.where(i < t2, b[...], c[...]))
    y = pick(y0_ref, y1_ref, y2_ref)
    x = pick(x0_ref, x1_ref_in, x2_ref)
    mixed = jnp.dot(y.astype(BF16), wo_ref[...], preferred_element_type=F32)
    x1 = _layer_norm(DEEPNORM_ALPHA * x + mixed, g_ref[...], b_ref[...])
    x1_ref[...] = x1
    x1b = x1.astype(BF16)

    logits = lax.dot_general(rw_ref[...], x1b, (((1,), (1,)), ((), ())), preferred_element_type=F32)
    scores = _sigmoid(logits)
    idxs, gates, chosen, e_iota = _route(scores, scores + rb_ref[...])
    gates = gates / jnp.sum(gates, axis=0, keepdims=True) * ROUTED_SCALE

    tm = scores.shape[1]
    ri = lax.broadcasted_iota(I32, (tm, tm), 0)
    ci = lax.broadcasted_iota(I32, (tm, tm), 1)
    before = jnp.where(ri < ci, 1.0, 0.0).astype(BF16)
    rank_all = jnp.dot(chosen.astype(BF16), before, preferred_element_type=F32) + cnt_scr[...]
    ranks = [jnp.sum(jnp.where(e_iota == idxs[j:j + 1, :], rank_all, 0.0), axis=0, keepdims=True)
             for j in range(TOP_K)]
    eidx_ref[...] = idxs.astype(I32)
    gate_ref[...] = gates
    rank_ref[...] = jnp.concatenate(ranks, axis=0).astype(I32)
    cnt_scr[...] = cnt_scr[...] + jnp.sum(chosen, axis=1, keepdims=True)
    cnt_ref[...] = cnt_scr[...]


def _seg_spec(tm, width, seg_tiles, k):
    first = sum(seg_tiles[:k])
    last = seg_tiles[k] - 1
    return pl.BlockSpec((tm, width), lambda i: (jnp.clip(i - first, 0, last), 0))


def _post_mixer(ys, xs, w_out, ln_g, ln_b, router_wt, router_b):
    tm = 256
    assert all(y.shape == x.shape and x.shape[0] % tm == 0 for y, x in zip(ys, xs))
    seg_tiles = tuple(x.shape[0] // tm for x in xs)
    rows = sum(x.shape[0] for x in xs)
    row_spec = lambda w: pl.BlockSpec((tm, w), lambda i: (i, 0))
    slot_spec = pl.BlockSpec((TOP_K, tm), lambda i: (0, i))
    seg_specs = [_seg_spec(tm, D_MODEL, seg_tiles, k) for k in range(3)]
    return pl.pallas_call(
        functools.partial(_post_mixer_kernel, seg_tiles=seg_tiles),
        out_shape=[
            jax.ShapeDtypeStruct((rows, D_MODEL), F32),
            jax.ShapeDtypeStruct((TOP_K, rows), I32),
            jax.ShapeDtypeStruct((TOP_K, rows), F32),
            jax.ShapeDtypeStruct((TOP_K, rows), I32),
            jax.ShapeDtypeStruct((N_EXPERTS, 1), F32),
        ],
        grid=(rows // tm,),
        in_specs=seg_specs + seg_specs + [_full_spec(w_out.shape), _full_spec(ln_g.shape),
                                          _full_spec(ln_b.shape), _full_spec(router_wt.shape),
                                          _full_spec(router_b.shape)],
        out_specs=[row_spec(D_MODEL), slot_spec, slot_spec, slot_spec, _full_spec((N_EXPERTS, 1))],
        scratch_shapes=[pltpu.VMEM((N_EXPERTS, 1), F32)],
        compiler_params=_cparams(1),
        name="post_mixer_router",
    )(*ys, *xs, w_out, ln_g, ln_b, router_wt, router_b)


def _cumsum_experts(lower, x):
    lo = jnp.broadcast_to((x & 63).astype(F32), (N_EXPERTS, LANES)).astype(BF16)
    hi = jnp.broadcast_to((x >> 6).astype(F32), (N_EXPERTS, LANES)).astype(BF16)
    return jnp.dot(lower, hi, preferred_element_type=F32) * 64.0 + jnp.dot(lower, lo, preferred_element_type=F32)


def _plan_kernel(cnt_ref, eidx_ref, rank_ref, dest_ref, iexp_ref, irow_ref, inb_ref, inz_ref, inext_ref,
                 iwslot_ref, pads_ref, start_scr, *, n_blk, n_item_pad):
    i = pl.program_id(0)

    @pl.when(i == 0)
    def _offsets():
        cnt = cnt_ref[...].astype(I32)
        nblk = (cnt + (MOE_BLOCK - 1)) >> (MOE_BLOCK.bit_length() - 1)
        nitem = (nblk + (ITEM_BLOCKS - 1)) >> (ITEM_BLOCKS.bit_length() - 1)
        ri = lax.broadcasted_iota(I32, (N_EXPERTS, N_EXPERTS), 0)
        ci = lax.broadcasted_iota(I32, (N_EXPERTS, N_EXPERTS), 1)
        lower = jnp.where(ci <= ri, 1.0, 0.0).astype(BF16)
        nblk_f = jnp.broadcast_to(nblk.astype(F32), (N_EXPERTS, LANES))
        nitem_f = jnp.broadcast_to(nitem.astype(F32), (N_EXPERTS, LANES))
        end_blk = _cumsum_experts(lower, nblk)
        end_item = _cumsum_experts(lower, nitem)
        start_blk = end_blk - nblk_f
        start_scr[...] = start_blk * float(MOE_BLOCK)
        item = lax.broadcasted_iota(I32, (1, n_item_pad), 1).astype(F32)
        done = jnp.where(end_item[:, 0:1] <= item, 1.0, 0.0)
        e_of = jnp.minimum(jnp.sum(done, axis=0, keepdims=True), float(N_EXPERTS - 1))
        hit = lax.broadcasted_iota(I32, (N_EXPERTS, n_item_pad), 0).astype(F32) == e_of
        pick = lambda col: jnp.sum(jnp.where(hit, col[:, 0:1], 0.0), axis=0, keepdims=True)
        k = item - pick(end_item - nitem_f)
        nb = jnp.clip(pick(nblk_f) - k * ITEM_BLOCKS, 0.0, float(ITEM_BLOCKS))
        n_used = end_blk[N_EXPERTS - 1:N_EXPERTS, 0:1]
        n_items = end_item[N_EXPERTS - 1:N_EXPERTS, 0:1]
        tail_blk = n_used + (item - n_items) * ITEM_BLOCKS
        nz = jnp.where(item >= n_items, jnp.clip(float(n_blk) - tail_blk, 0.0, float(ITEM_BLOCKS)), 0.0)
        blk0 = jnp.where(item >= n_items, tail_blk, pick(start_blk) + k * ITEM_BLOCKS)
        iexp_ref[...] = e_of.astype(I32)
        irow_ref[...] = (jnp.minimum(blk0, float(n_blk - 1)) * MOE_BLOCK).astype(I32)
        inb_ref[...] = nb.astype(I32)
        inz_ref[...] = nz.astype(I32)
        owner_no = _cumsum_experts(lower, jnp.minimum(nitem, 1))
        no_w = pick(owner_no)
        upto = jnp.sum(jnp.where(owner_no[:, 0:1] <= no_w, 1.0, 0.0), axis=0, keepdims=True)
        inext_ref[...] = jnp.where(upto < float(N_EXPERTS), upto, -1.0).astype(I32)
        iwslot_ref[...] = no_w.astype(I32) & 1
        cnt_f = jnp.broadcast_to(cnt.astype(F32), (N_EXPERTS, LANES))
        pad_start = jnp.transpose(start_blk * float(MOE_BLOCK) + cnt_f)[0:1]
        pad_len = jnp.transpose(nblk_f * float(MOE_BLOCK) - cnt_f)[0:1]
        used = jnp.broadcast_to(n_used, (1, N_EXPERTS))
        pads_ref[...] = jnp.concatenate(
            [pad_start, pad_len, used, jnp.zeros((SUBLANES - 3, N_EXPERTS), F32)], axis=0).astype(I32)

    start = start_scr[:, 0:1]
    eidx = eidx_ref[...]
    tm = eidx.shape[1]
    e_iota = lax.broadcasted_iota(I32, (N_EXPERTS, tm), 0)
    rows = [jnp.sum(jnp.where(e_iota == eidx[j:j + 1, :], start, 0.0), axis=0, keepdims=True)
            for j in range(TOP_K)]
    dest_ref[...] = jnp.concatenate(rows, axis=0).astype(I32) + rank_ref[...]


def _plan(counts, eidx, rank, n_blk, n_item_pad):
    rows = eidx.shape[1]
    tm = ROW_DMA_TILE
    slot_spec = pl.BlockSpec((TOP_K, tm), lambda i: (0, i))
    item_shape = jax.ShapeDtypeStruct((1, n_item_pad), I32)
    return pl.pallas_call(
        functools.partial(_plan_kernel, n_blk=n_blk, n_item_pad=n_item_pad),
        out_shape=([jax.ShapeDtypeStruct((rows // tm, TOP_K, tm), I32)] + [item_shape] * N_ITEM_TABLES
                   + [jax.ShapeDtypeStruct((SUBLANES, N_EXPERTS), I32)]),
        grid=(rows // tm,),
        in_specs=[_full_spec((N_EXPERTS, 1)), slot_spec, slot_spec],
        out_specs=([pl.BlockSpec((None, TOP_K, tm), lambda i: (i, 0, 0))]
                   + [_full_spec((1, n_item_pad))] * N_ITEM_TABLES + [_full_spec((SUBLANES, N_EXPERTS))]),
        scratch_shapes=[pltpu.VMEM((N_EXPERTS, LANES), F32)],
        compiler_params=_cparams(1),
        name="dispatch_plan",
    )(counts, eidx, rank)


PAD_PIECES = (64, 32, 16, 8)


def _dispatch_kernel(pads_ref, dest_ref, x_ref, xs_ref, dest_smem, zbuf, sem_idx, sem_rows, sem_zero, *, n_blk_total):
    i = pl.program_id(0)
    tm = x_ref.shape[0]
    idx_copy = pltpu.make_async_copy(dest_ref.at[0], dest_smem, sem_idx)
    idx_copy.start()
    idx_copy.wait()

    def row_copy(tok, slot):
        return pltpu.make_async_copy(x_ref.at[pl.ds(tok, 1), :], xs_ref.at[pl.ds(dest_smem[slot, tok], 1), :],
                                     sem_rows)

    def start_body(tok, carry):
        for slot in range(TOP_K):
            row_copy(tok, slot).start(priority=slot % 2)
        return carry

    def wait_body(tok, carry):
        for slot in range(TOP_K):
            row_copy(tok, slot).wait()
        return carry

    def zero_copy(row0, n):
        return pltpu.make_async_copy(zbuf.at[pl.ds(0, n), :], xs_ref.at[pl.ds(row0, n), :], sem_zero)

    def for_each_zero_copy(fn):
        def pad_body(e, carry):
            row0 = pads_ref[e]
            n = pads_ref[N_EXPERTS + e]
            head = (-row0) & (SUBLANES - 1)
            for r in range(SUBLANES - 1):
                @pl.when(r < head)
                def _():
                    fn(zero_copy(row0 + r, 1))
            body = n - head
            for piece in PAD_PIECES:
                @pl.when((body & piece) != 0)
                def _():
                    fn(zero_copy(pl.multiple_of(row0 + head + (body & ~(2 * piece - 1)), SUBLANES), piece))
            return carry

        def tail_body(b, carry):
            fn(zero_copy(pl.multiple_of(b * MOE_BLOCK, MOE_BLOCK), MOE_BLOCK))
            return carry

        lax.fori_loop(0, N_EXPERTS, pad_body, 0)
        lax.fori_loop(pads_ref[2 * N_EXPERTS], n_blk_total, tail_body, 0)

    lax.fori_loop(0, tm, start_body, 0)

    @pl.when(i == pl.num_programs(0) - 1)
    def _zero_fill():
        zbuf[...] = jnp.zeros_like(zbuf)
        for_each_zero_copy(lambda c: c.start())
        for_each_zero_copy(lambda c: c.wait())

    lax.fori_loop(0, tm, wait_body, 0)


def _dispatch(pads, dest, x1, n_blk_total):
    rows, width = x1.shape
    tm = ROW_DMA_TILE
    grid_spec = pltpu.PrefetchScalarGridSpec(
        num_scalar_prefetch=1,
        grid=(rows // tm,),
        in_specs=[pl.BlockSpec((1, TOP_K, tm), lambda i, p: (i, 0, 0)),
                  pl.BlockSpec((tm, width), lambda i, p: (i, 0))],
        out_specs=pl.BlockSpec(memory_space=pl.ANY),
        scratch_shapes=[pltpu.SMEM((TOP_K, tm), I32), pltpu.VMEM((MOE_BLOCK, width), F32),
                        pltpu.SemaphoreType.DMA, pltpu.SemaphoreType.DMA, pltpu.SemaphoreType.DMA],
    )
    return pl.pallas_call(
        functools.partial(_dispatch_kernel, n_blk_total=n_blk_total),
        out_shape=jax.ShapeDtypeStruct((n_blk_total * MOE_BLOCK, width), F32),
        grid_spec=grid_spec,
        compiler_params=_cparams(1),
        name="moe_dispatch",
    )(pads, dest, x1)


def _experts_kernel(iexp_ref, irow_ref, inb_ref, inz_ref, inext_ref, iwslot_ref, xs_ref, wg_ref, wu_ref, wd_ref,
                    ys_ref, wg_b, wu_b, wd_b, ybuf, sems, wg_f, wu_f, wd_f, wsems):
    w = pl.program_id(0)
    n_items = pl.num_programs(0)
    slot = w & 1

    def weight_copies(expert, wslot):
        return [pltpu.make_async_copy(src.at[expert], dst.at[wslot], wsems.at[wslot, k])
                for k, (src, dst) in enumerate(((wg_ref, wg_f), (wu_ref, wu_f), (wd_ref, wd_f)))]

    @pl.when(w == 0)
    def _first_weights():
        for c in weight_copies(iexp_ref[0], iwslot_ref[0]):
            c.start()

    def out_copy(step, piece):
        row0 = pl.multiple_of(irow_ref[step] + piece * MOE_BLOCK, MOE_BLOCK)
        return pltpu.make_async_copy(ybuf.at[step & 1, pl.ds(piece * MOE_BLOCK, MOE_BLOCK), :],
                                     ys_ref.at[pl.ds(row0, MOE_BLOCK), :], sems.at[step & 1, piece])

    def n_pieces(step):
        return inb_ref[step] + inz_ref[step]

    def wait_step(step):
        for piece in range(ITEM_BLOCKS):
            @pl.when(n_pieces(step) > piece)
            def _():
                out_copy(step, piece).wait()

    @pl.when(w >= 2)
    def _():
        wait_step(jnp.maximum(w - 2, 0))

    new_expert = (w == 0) | (iexp_ref[w] != iexp_ref[jnp.maximum(w - 1, 0)])

    @pl.when((inb_ref[w] > 0) & new_expert)
    def _load_weights():
        wslot = iwslot_ref[w]
        for c in weight_copies(iexp_ref[w], wslot):
            c.wait()
        wg_b[...] = wg_f[wslot].astype(BF16)
        wu_b[...] = wu_f[wslot].astype(BF16)
        wd_b[...] = wd_f[wslot].astype(BF16)

        @pl.when(inext_ref[w] >= 0)
        def _prefetch():
            for c in weight_copies(inext_ref[w], 1 - wslot):
                c.start()

    for part in range(ITEM_BLOCKS * MOE_BLOCK // FFN_ROWS):
        @pl.when(inb_ref[w] > part * (FFN_ROWS // MOE_BLOCK))
        def _ffn():
            part_rows = pl.ds(part * FFN_ROWS, FFN_ROWS)
            x = xs_ref[part_rows, :].astype(BF16)
            dot = functools.partial(jnp.dot, preferred_element_type=F32)
            hidden = (_silu(dot(x, wg_b[...])) * dot(x, wu_b[...])).astype(BF16)
            ybuf[slot, part_rows, :] = dot(hidden, wd_b[...])

    @pl.when(inz_ref[w] > 0)
    def _zero_tail():
        ybuf[slot] = jnp.zeros(ybuf.shape[1:], F32)

    for piece in range(ITEM_BLOCKS):
        @pl.when(n_pieces(w) > piece)
        def _():
            out_copy(w, piece).start()

    @pl.when(w == n_items - 1)
    def _drain():
        @pl.when(w >= 1)
        def _():
            wait_step(jnp.maximum(w - 1, 0))
        wait_step(w)


def _experts(items, n_items, xs, w_gate, w_up, w_down, n_rows_out):
    item_rows = ITEM_BLOCKS * MOE_BLOCK
    hbm = pl.BlockSpec(memory_space=pl.ANY)
    grid_spec = pltpu.PrefetchScalarGridSpec(
        num_scalar_prefetch=len(items),
        grid=(n_items,),
        in_specs=[
            pl.BlockSpec((pl.Element(item_rows), pl.Element(D_MODEL)),
                         lambda w, ie, ir, *_: (pl.multiple_of(ir[w], MOE_BLOCK), 0)),
            hbm, hbm, hbm,
        ],
        out_specs=hbm,
        scratch_shapes=[pltpu.VMEM((D_MODEL, EXPERT_DIM), BF16), pltpu.VMEM((D_MODEL, EXPERT_DIM), BF16),
                        pltpu.VMEM((EXPERT_DIM, D_MODEL), BF16), pltpu.VMEM((2, item_rows, D_MODEL), F32),
                        pltpu.SemaphoreType.DMA((2, ITEM_BLOCKS)),
                        pltpu.VMEM((2, D_MODEL, EXPERT_DIM), F32), pltpu.VMEM((2, D_MODEL, EXPERT_DIM), F32),
                        pltpu.VMEM((2, EXPERT_DIM, D_MODEL), F32), pltpu.SemaphoreType.DMA((2, 3))],
    )
    return pl.pallas_call(
        _experts_kernel,
        out_shape=jax.ShapeDtypeStruct((n_rows_out, D_MODEL), F32),
        grid_spec=grid_spec,
        compiler_params=_cparams(1),
        name="moe_experts",
    )(*items, xs, w_gate, w_up, w_down)


def _combine_kernel(dest_ref, gate_ref, x1_ref, sg_ref, su_ref, sd_ref, g_ref, b_ref, ys_ref,
                    out0_ref, out1_ref, dest_smem, ybuf, sem_idx, sem_rows, *, seg_tiles):
    i = pl.program_id(0)
    tm = x1_ref.shape[0]
    idx_copy = pltpu.make_async_copy(dest_ref.at[0], dest_smem, sem_idx)
    idx_copy.start()
    idx_copy.wait()

    def row_copy(tok, slot):
        return pltpu.make_async_copy(ys_ref.at[pl.ds(dest_smem[slot, tok], 1), :],
                                     ybuf.at[slot, pl.ds(tok, 1), :], sem_rows)

    def start_body(tok, carry):
        for slot in range(TOP_K):
            row_copy(tok, slot).start(priority=slot % 2)
        return carry

    def wait_body(tok, carry):
        for slot in range(TOP_K):
            row_copy(tok, slot).wait()
        return carry

    lax.fori_loop(0, tm, start_body, 0)

    x1 = x1_ref[...]
    x1b = x1.astype(BF16)
    dot = functools.partial(jnp.dot, preferred_element_type=F32)
    hidden = (_silu(dot(x1b, sg_ref[...])) * dot(x1b, su_ref[...])).astype(BF16)
    acc = dot(hidden, sd_ref[...])

    lax.fori_loop(0, tm, wait_body, 0)
    gates = gate_ref[...]
    routed = None
    for slot in range(TOP_K):
        term = gates[:, slot:slot + 1] * ybuf[slot]
        routed = term if routed is None else routed + term
    out = _layer_norm(DEEPNORM_ALPHA * x1 + (routed + acc), g_ref[...], b_ref[...])

    @pl.when(i < seg_tiles[0])
    def _():
        out0_ref[...] = out

    @pl.when((i >= seg_tiles[0]) & (i < seg_tiles[0] + seg_tiles[1]))
    def _():
        out1_ref[...] = out


def _combine(dest, gates, x1, sh_gate, sh_up, sh_down, ln_g, ln_b, ys, seg_rows):
    rows = x1.shape[0]
    tm = ROW_DMA_TILE
    seg_tiles = tuple(r // tm for r in seg_rows)
    assert sum(seg_rows) == rows and all(r % tm == 0 for r in seg_rows)
    return pl.pallas_call(
        functools.partial(_combine_kernel, seg_tiles=seg_tiles),
        out_shape=[jax.ShapeDtypeStruct((seg_rows[0], D_MODEL), F32),
                   jax.ShapeDtypeStruct((seg_rows[1], D_MODEL), F32)],
        grid=(rows // tm,),
        in_specs=[pl.BlockSpec((1, TOP_K, tm), lambda i: (i, 0, 0)),
                  pl.BlockSpec((tm, TOP_K), lambda i: (i, 0)),
                  pl.BlockSpec((tm, D_MODEL), lambda i: (i, 0)),
                  _full_spec(sh_gate.shape), _full_spec(sh_up.shape), _full_spec(sh_down.shape),
                  _full_spec(ln_g.shape), _full_spec(ln_b.shape),
                  pl.BlockSpec(memory_space=pl.ANY)],
        out_specs=[_seg_spec(tm, D_MODEL, seg_tiles, 0), _seg_spec(tm, D_MODEL, seg_tiles, 1)],
        scratch_shapes=[pltpu.SMEM((TOP_K, tm), I32), pltpu.VMEM((TOP_K, tm, D_MODEL), F32),
                        pltpu.SemaphoreType.DMA, pltpu.SemaphoreType.DMA],
        compiler_params=_cparams(1),
        name="moe_combine",
    )(dest, gates, x1, sh_gate, sh_up, sh_down, ln_g, ln_b, ys)


def _pair_blockdiag(w):
    nb, bd, _ = w.shape
    z = jnp.zeros((bd, bd), w.dtype)
    pairs = [jnp.block([[w[2 * p], z], [z, w[2 * p + 1]]]) for p in range(nb // 2)]
    return jnp.stack(pairs, axis=0)


def _lane_row(vals, offset):
    return jnp.zeros((1, LANES), F32).at[0, offset:offset + vals.shape[0]].set(vals.astype(F32))


def kernel(x_prompt, x_sample, state_gdn, state_gdn_conv, state_lru, state_lru_conv, meta_tokens, w_in, gdn_conv_w, gdn_a_log, gdn_dt_bias, gdn_norm_w, lru_conv_w, lru_conv_b, lru_gate_a_w, lru_gate_a_b, lru_gate_x_w, lru_gate_x_b, lru_lambda, lru_norm_w, w_out, ln_mix_g, ln_mix_b, router_w, router_bias, expert_w_gate, expert_w_up, expert_w_down, shared_w_gate, shared_w_up, shared_w_down, ln_ffn_g, ln_ffn_b):
    depth = w_in.shape[0]
    assert depth == 1
    batch, seq, _ = x_prompt.shape
    dec_batch, dec_seq, _ = x_sample.shape
    n_prompt = batch * seq
    n_sample = dec_batch * dec_seq
    assert seq % PROMPT_TILE == 0 and n_sample % 256 == 0 and dec_seq == SUBLANES and N_META >= HIST

    x_segs = (x_prompt.reshape(n_prompt, D_MODEL), x_sample.reshape(n_sample, D_MODEL),
              jnp.concatenate([meta_tokens.astype(F32), jnp.zeros((META_ROWS - N_META, D_MODEL), F32)], axis=0))
    seg_rows = tuple(x.shape[0] for x in x_segs)
    rows = sum(seg_rows)
    l = 0
    w = w_in[l]
    o_ba = 4 * GDN_WIDTH
    o_lru = o_ba + 2 * GDN_HEADS
    w_cat = jnp.concatenate([w[:, :o_ba], w[:, o_lru:], w[:, o_ba:o_lru],
                             jnp.zeros((D_MODEL, LANES - 2 * GDN_HEADS), w.dtype)], axis=1).astype(BF16)
    mixer_wts = (
        gdn_conv_w[l], lru_conv_w[l], lru_conv_b[l][None], _pair_blockdiag(lru_gate_a_w[l]),
        _pair_blockdiag(lru_gate_x_w[l]), lru_gate_a_b[l][None], lru_gate_x_b[l][None], lru_lambda[l][None],
        lru_norm_w[l][None], gdn_norm_w[l][None], _lane_row(gdn_a_log[l], GDN_HEADS),
        _lane_row(gdn_dt_bias[l], GDN_HEADS),
    )

    proj_p, proj_s, proj_m = (_proj(x, w_cat) for x in x_segs)

    gdn_shape = (GDN_HEADS, HEAD_DIM, HEAD_DIM)
    zeros_m = (jnp.zeros((1, GDN_HEADS * HEAD_DIM, HEAD_DIM), F32), jnp.zeros((1, HIST, QKV), F32),
               jnp.zeros((1, 1, LRU_WIDTH), F32), jnp.zeros((1, HIST, LRU_WIDTH), F32))
    y_mix_m, *states_m = _mixer(
        proj_m, zeros_m, mixer_wts, name="mixer_meta", n_seq=1, seq=META_ROWS, ns=1, ts=GDN_CHUNK, chunk=GDN_CHUNK,
        t_valid=N_META, reset_first=True)
    y_mix_p, gdn_p, gconv_p, lru_p, lconv_p = _mixer(
        proj_p, states_m, mixer_wts, name="mixer_prompt", n_seq=batch, seq=seq, ns=1, ts=PROMPT_TILE,
        chunk=GDN_CHUNK, t_valid=seq, reset_first=False, shared_init=True)
    states_s = (state_gdn[l].reshape(dec_batch, GDN_HEADS * HEAD_DIM, HEAD_DIM), state_gdn_conv[l],
                state_lru[l][:, None, :], state_lru_conv[l])
    y_mix_s, gdn_s, gconv_s, lru_s, lconv_s = _mixer(
        proj_s, states_s, mixer_wts, name="mixer_sample", n_seq=dec_batch, seq=dec_seq, ns=SAMPLE_SEQS_PER_STEP,
        ts=dec_seq, chunk=dec_seq, t_valid=dec_seq, reset_first=False)

    x1, eidx, gates, rank, counts = _post_mixer(
        (y_mix_p, y_mix_s, y_mix_m), x_segs, w_out[l].astype(BF16), ln_mix_g[l][None], ln_mix_b[l][None],
        router_w[l].T.astype(BF16), router_bias[l][:, None])

    n_pairs = rows * TOP_K
    n_blk = (n_pairs + N_EXPERTS * (MOE_BLOCK - 1) + MOE_BLOCK - 1) // MOE_BLOCK
    n_items = -(-n_blk // ITEM_BLOCKS) + N_EXPERTS
    n_item_pad = -(-n_items // LANES) * LANES
    dest, *items, pads = _plan(counts, eidx, rank, n_blk, n_item_pad)
    xs = _dispatch(pads.reshape(-1), dest, x1, n_blk + ITEM_BLOCKS - 1)
    ys = _experts([t[0] for t in items], n_items, xs, expert_w_gate[l], expert_w_up[l], expert_w_down[l],
                  n_blk * MOE_BLOCK)
    y_p, y_s = _combine(dest, gates.T, x1, shared_w_gate[l].astype(BF16), shared_w_up[l].astype(BF16),
                        shared_w_down[l].astype(BF16), ln_ffn_g[l][None], ln_ffn_b[l][None], ys, seg_rows)

    y_prompt = y_p.reshape(batch, seq, D_MODEL)
    y_sample = y_s.reshape(dec_batch, dec_seq, D_MODEL)
    return (y_prompt, y_sample, gdn_p.reshape(1, batch, *gdn_shape), gconv_p[None],
            lru_p.reshape(1, batch, LRU_WIDTH), lconv_p[None], gdn_s.reshape(1, dec_batch, *gdn_shape),
            gconv_s[None], lru_s.reshape(1, dec_batch, LRU_WIDTH), lconv_s[None])
```
